```python
import math
import jax, jax.numpy as jnp
from jax import lax
import numpy as np

D_MODEL = 2048
BATCH = 4
SEQ = 2048
DEPTH = 4

GRID_W = 64
CTX_LEN = 256
N_MOD = 6
EPS = 1e-6

N_HEADS = 16
KV_HEADS = 4
GROUP = N_HEADS // KV_HEADS
HEAD_DIM = 128
WINDOW = 128
ATTN_BLOCK = 128
ROPE_BASE = 10000.0
ROPE_FREQS = HEAD_DIM // 4
A_Q = N_HEADS * HEAD_DIM
A_KV = KV_HEADS * HEAD_DIM
A_QKV = A_Q + 2 * A_KV

DN_HK = 16
DN_HV = 32
DN_DK = 128
DN_DV = 128
DN_K = DN_HK * DN_DK
DN_V = DN_HV * DN_DV
DN_QKV = 2 * DN_K + DN_V
DN_IN = DN_QKV + DN_V + 4 * DN_HV
SHORT_CONV = 5
CHUNK = 64

D_FF = 5632
N_EXPERTS = 8
TOP_K = 2
D_FF_EXPERT = 5632

N_EVEN = (DEPTH + 1) // 2
N_ODD = DEPTH // 2

kernel_name = "hybrid_swa_gdn_moe_diffusion_trunk"


def _rmsnorm(x, g):
    xf = x.astype(jnp.float32)
    y = xf * lax.rsqrt(jnp.mean(xf * xf, axis=-1, keepdims=True) + EPS)
    return (y * g.astype(jnp.float32)).astype(x.dtype)


def _modulate(h, shift, scale):
    return h * (1 + scale[:, None]) + shift[:, None]


def _l2norm(x):
    return x * lax.rsqrt(jnp.sum(x * x, axis=-1, keepdims=True) + EPS)


def _axial_angles(n_tokens):
    rows_n = n_tokens // GRID_W
    row = jnp.repeat(jnp.arange(rows_n, dtype=jnp.float32), GRID_W)
    col = jnp.tile(jnp.arange(GRID_W, dtype=jnp.float32), rows_n)
    inv = ROPE_BASE ** (-jnp.arange(ROPE_FREQS, dtype=jnp.float32) / ROPE_FREQS)
    return jnp.stack([row[:, None] * inv, col[:, None] * inv], axis=1)


def _rope_axial(x, ang):
    shp = x.shape
    xr = x.reshape(shp[:-1] + (2, 2, ROPE_FREQS))
    a = ang.reshape((shp[1],) + (1,) * (x.ndim - 3) + (2, ROPE_FREQS))
    cos, sin = jnp.cos(a), jnp.sin(a)
    x1, x2 = xr[..., 0, :], xr[..., 1, :]
    return jnp.stack([x1 * cos - x2 * sin, x2 * cos + x1 * sin], axis=-2).reshape(shp)


def _window_gqa(h_lat, h_ctx, w_qkv, w_o, sink, need_ctx):
    B, S, _ = h_lat.shape
    Lc = h_ctx.shape[1]
    nb = S // ATTN_BLOCK
    nw = 3 * ATTN_BLOCK
    f32 = jnp.float32
    scale = HEAD_DIM ** -0.5

    def project(h):
        p = (h @ w_qkv).astype(f32)
        Bh, L = p.shape[:2]
        q = p[..., :A_Q].reshape(Bh, L, KV_HEADS, GROUP, HEAD_DIM) * scale
        k = p[..., A_Q:A_Q + A_KV].reshape(Bh, L, KV_HEADS, HEAD_DIM)
        v = p[..., A_Q + A_KV:].reshape(Bh, L, KV_HEADS, HEAD_DIM)
        return q, k, v

    q_l, k_l, v_l = project(h_lat)
    q_c, k_c, v_c = project(h_ctx)
    ang = _axial_angles(S)
    q_l = _rope_axial(q_l, ang)
    k_l = _rope_axial(k_l, ang)
    sink_l = sink.astype(f32).reshape(KV_HEADS, GROUP, 1, 1)

    qb = q_l.reshape(B, nb, ATTN_BLOCK, KV_HEADS, GROUP, HEAD_DIM)

    def band(t):
        tp = jnp.pad(t, ((0, 0), (ATTN_BLOCK, ATTN_BLOCK), (0, 0), (0, 0)))
        tp = tp.reshape(B, nb + 2, ATTN_BLOCK, KV_HEADS, HEAD_DIM)
        return jnp.concatenate([tp[:, :-2], tp[:, 1:-1], tp[:, 2:]], axis=2)

    k_b, v_b = band(k_l), band(v_l)
    s_win = jnp.einsum('bnqkgd,bnjkd->bnkgqj', qb, k_b)
    blk = jnp.arange(nb)[:, None]
    q_pos = blk * ATTN_BLOCK + jnp.arange(ATTN_BLOCK)[None]
    k_pos = (blk - 1) * ATTN_BLOCK + jnp.arange(nw)[None]
    rel = k_pos[:, None, :] - q_pos[:, :, None]
    valid = (jnp.abs(rel) <= WINDOW) & (k_pos[:, None, :] >= 0) & (k_pos[:, None, :] < S)
    s_win = jnp.where(valid[None, :, None, None], s_win, -jnp.inf)
    s_ctx = jnp.einsum('bnqkgd,bjkd->bnkgqj', qb, k_c)
    sink_b = jnp.broadcast_to(sink_l, s_win.shape[:-1] + (1,))
    p = jax.nn.softmax(jnp.concatenate([s_win, s_ctx, sink_b], axis=-1), axis=-1)
    o_l = (jnp.einsum('bnkgqj,bnjkd->bnqkgd', p[..., :nw], v_b)
           + jnp.einsum('bnkgqj,bjkd->bnqkgd', p[..., nw:nw + Lc], v_c))
    y_l = o_l.reshape(B, S, A_Q).astype(h_lat.dtype) @ w_o
    if not need_ctx:
        return y_l, None

    s_cc = jnp.einsum('bqkgd,bjkd->bkgqj', q_c, k_c)
    sink_c = jnp.broadcast_to(sink_l, s_cc.shape[:-1] + (1,))
    p_c = jax.nn.softmax(jnp.concatenate([s_cc, sink_c], axis=-1), axis=-1)
    o_c = jnp.einsum('bkgqj,bjkd->bqkgd', p_c[..., :Lc], v_c)
    y_c = o_c.reshape(B, Lc, A_Q).astype(h_ctx.dtype) @ w_o
    return y_l, y_c


def _short_conv(x, w):
    ch = x.shape[-1]
    pad = SHORT_CONV // 2
    return lax.conv_general_dilated(x, w[:, None, :].astype(x.dtype), window_strides=(1,),
                                    padding=[(pad, pad)], dimension_numbers=('NWC', 'WIO', 'NWC'),
                                    feature_group_count=ch)


def _chunked_gated_delta(q, k, v, g, beta):
    Bn, L, H, _ = q.shape
    dv = v.shape[-1]
    n = L // CHUNK

    def chunks(t):
        return t.reshape(Bn, n, CHUNK, H, t.shape[-1]).transpose(1, 0, 3, 2, 4)

    q, k, v = chunks(q), chunks(k), chunks(v)
    g = jnp.cumsum(g.reshape(Bn, n, CHUNK, H).transpose(1, 0, 3, 2), axis=-1)
    beta = beta.reshape(Bn, n, CHUNK, H).transpose(1, 0, 3, 2)
    incl = jnp.tril(jnp.ones((CHUNK, CHUNK), bool))
    strict = jnp.tril(jnp.ones((CHUNK, CHUNK), bool), -1)
    decay = jnp.exp(jnp.where(incl, g[..., :, None] - g[..., None, :], -jnp.inf))
    kb = k * beta[..., None]
    m = jnp.where(strict, jnp.einsum('nbhik,nbhjk->nbhij', kb, k) * decay, 0.0)
    eg = jnp.exp(g)
    rhs = jnp.concatenate([v * beta[..., None], kb * eg[..., None]], axis=-1)
    sol = lax.linalg.triangular_solve(m + jnp.eye(CHUNK, dtype=m.dtype), rhs, left_side=True,
                                      lower=True, unit_diagonal=True)
    u, w = sol[..., :dv], sol[..., dv:]
    qk = jnp.einsum('nbhik,nbhjk->nbhij', q, k) * decay
    qd = q * eg[..., None]
    kd = k * jnp.exp(g[..., -1:] - g)[..., None]
    gl = jnp.exp(g[..., -1])[..., None, None]

    def step(s, xs):
        qk_i, u_i, w_i, qd_i, kd_i, gl_i = xs
        v_new = u_i - jnp.einsum('bhck,bhkv->bhcv', w_i, s)
        o = jnp.einsum('bhck,bhkv->bhcv', qd_i, s) + jnp.einsum('bhij,bhjv->bhiv', qk_i, v_new)
        s = s * gl_i + jnp.einsum('bhck,bhcv->bhkv', kd_i, v_new)
        return s, o

    s0 = jnp.zeros((Bn, H, q.shape[-1], dv), q.dtype)
    _, o = lax.scan(step, s0, (qk, u, w, qd, kd, gl))
    return o.transpose(1, 0, 3, 2, 4).reshape(Bn, L, H, dv)


def _gated_deltanet(h_lat, h_ctx, w_in, conv_w, a_log, dt_bias, norm_w, w_o, need_ctx):
    B, S, _ = h_lat.shape
    Lc = h_ctx.shape[1]
    L = Lc + S
    f32 = jnp.float32

    def project(h):
        p = h @ w_in
        qkv = jax.nn.silu(_short_conv(p[..., :DN_QKV], conv_w)).astype(f32)
        z = p[..., DN_QKV:DN_QKV + DN_V]
        ab = p[..., DN_QKV + DN_V:].astype(f32)
        return qkv, z, ab

    qkv_c, z_c, ab_c = project(h_ctx)
    qkv_l, z_l, ab_l = project(h_lat)
    rev = lambda t: t[:, ::-1]
    fwd = lambda tc, tl: jnp.concatenate([tc, tl], axis=1)
    bwd = lambda tc, tl: jnp.concatenate([rev(tc), rev(tl)], axis=1)
    qkv = jnp.concatenate([fwd(qkv_c, qkv_l), bwd(qkv_c, qkv_l)], axis=0)
    ab = jnp.stack([fwd(ab_c[..., :2 * DN_HV], ab_l[..., :2 * DN_HV]),
                    bwd(ab_c[..., 2 * DN_HV:], ab_l[..., 2 * DN_HV:])])
    a, b = ab[..., :DN_HV], ab[..., DN_HV:]
    g = -jnp.exp(a_log.astype(f32))[:, None, None] * jax.nn.softplus(a + dt_bias.astype(f32)[:, None, None])
    beta = jax.nn.sigmoid(b)
    rep = DN_HV // DN_HK
    q = jnp.repeat(_l2norm(qkv[..., :DN_K].reshape(2 * B, L, DN_HK, DN_DK)), rep, axis=2) * DN_DK ** -0.5
    k = jnp.repeat(_l2norm(qkv[..., DN_K:2 * DN_K].reshape(2 * B, L, DN_HK, DN_DK)), rep, axis=2)
    v = qkv[..., 2 * DN_K:].reshape(2 * B, L, DN_HV, DN_DV)
    o = _chunked_gated_delta(q, k, v, g.reshape(2 * B, L, DN_HV), beta.reshape(2 * B, L, DN_HV))
    o = o.reshape(2, B, L, DN_HV, DN_DV)
    o_f, o_b = o[0], o[1]

    def readout(o_seq, z):
        y = _rmsnorm(o_seq, norm_w) * jax.nn.silu(z.astype(f32)).reshape(o_seq.shape)
        return y.reshape(z.shape).astype(z.dtype) @ w_o

    y_l = readout(o_f[:, Lc:] + rev(o_b[:, Lc:]), z_l)
    if not need_ctx:
        return y_l, None
    y_c = readout(o_f[:, :Lc] + rev(o_b[:, :Lc]), z_c)
    return y_l, y_c


def _swiglu(h, w_gate, w_up, w_down):
    return (jax.nn.silu(h @ w_gate) * (h @ w_up)) @ w_down


def _moe_swiglu(h, router, w_gate, w_up, w_down):
    logits = (h @ router).astype(jnp.float32)
    top_v, top_i = lax.top_k(logits, TOP_K)
    top_w = jax.nn.softmax(top_v, axis=-1)
    gates = jnp.sum(jax.nn.one_hot(top_i, N_EXPERTS, dtype=jnp.float32) * top_w[..., None], axis=1)
    y = jnp.zeros(h.shape, jnp.float32)
    for e in range(N_EXPERTS):
        y = y + gates[:, e:e + 1] * _swiglu(h, w_gate[e], w_up[e], w_down[e]).astype(jnp.float32)
    return y.astype(h.dtype)


def setup_inputs(seed: int = 0) -> dict:
    key = jax.random.key(seed)
    ks = jax.random.split(key, 24)
    f32 = jnp.float32

    def nrm(k, shape, fan_in, mult=1.0):
        return jax.random.normal(k, shape, f32) * (mult * fan_in ** -0.5)

    x = jax.random.normal(ks[0], (BATCH, SEQ, D_MODEL), f32)
    c = jax.random.normal(ks[1], (BATCH, D_MODEL), f32)
    ctx = jax.random.normal(ks[2], (BATCH, CTX_LEN, D_MODEL), f32)
    c_ctx = jax.random.normal(ks[3], (D_MODEL,), f32)
    mod_w = nrm(ks[4], (DEPTH, D_MODEL, N_MOD * D_MODEL), D_MODEL, 0.5)
    mod_b = 0.02 * jax.random.normal(ks[5], (DEPTH, N_MOD * D_MODEL), f32)
    norm_g = 1.0 + 0.02 * jax.random.normal(ks[6], (DEPTH, 4, D_MODEL), f32)
    attn_w_qkv = nrm(ks[7], (N_EVEN, D_MODEL, A_QKV), D_MODEL)
    attn_w_o = nrm(ks[8], (N_EVEN, A_Q, D_MODEL), A_Q)
    attn_sink = 0.5 * jax.random.normal(ks[9], (N_EVEN, N_HEADS), f32)
    dn_w_in = nrm(ks[10], (N_ODD, D_MODEL, DN_IN), D_MODEL)
    dn_conv = nrm(ks[11], (N_ODD, SHORT_CONV, DN_QKV), SHORT_CONV)
    dn_a_log = jnp.log(jax.random.uniform(ks[12], (N_ODD, 2, DN_HV), f32, 1.0, 16.0))
    dt = jnp.exp(jax.random.uniform(ks[13], (N_ODD, 2, DN_HV), f32, math.log(1e-3), math.log(1e-1)))
    dn_dt_bias = dt + jnp.log(-jnp.expm1(-dt))
    dn_norm_g = 1.0 + 0.02 * jax.random.normal(ks[14], (N_ODD, DN_DV), f32)
    dn_w_o = nrm(ks[15], (N_ODD, DN_V, D_MODEL), DN_V)
    ffn_w_gate = nrm(ks[16], (N_EVEN, D_MODEL, D_FF), D_MODEL)
    ffn_w_up = nrm(ks[17], (N_EVEN, D_MODEL, D_FF), D_MODEL)
    ffn_w_down = nrm(ks[18], (N_EVEN, D_FF, D_MODEL), D_FF)
    moe_router = nrm(ks[19], (N_ODD, D_MODEL, N_EXPERTS), D_MODEL)
    moe_w_gate = nrm(ks[20], (N_ODD, N_EXPERTS, D_MODEL, D_FF_EXPERT), D_MODEL)
    moe_w_up = nrm(ks[21], (N_ODD, N_EXPERTS, D_MODEL, D_FF_EXPERT), D_MODEL)
    moe_w_down = nrm(ks[22], (N_ODD, N_EXPERTS, D_FF_EXPERT, D_MODEL), D_FF_EXPERT)
    return {"x": x, "c": c, "ctx": ctx, "c_ctx": c_ctx, "mod_w": mod_w, "mod_b": mod_b,
            "norm_g": norm_g, "attn_w_qkv": attn_w_qkv, "attn_w_o": attn_w_o, "attn_sink": attn_sink,
            "dn_w_in": dn_w_in, "dn_conv": dn_conv, "dn_a_log": dn_a_log, "dn_dt_bias": dn_dt_bias,
            "dn_norm_g": dn_norm_g, "dn_w_o": dn_w_o, "ffn_w_gate": ffn_w_gate, "ffn_w_up": ffn_w_up,
            "ffn_w_down": ffn_w_down, "moe_router": moe_router, "moe_w_gate": moe_w_gate,
            "moe_w_up": moe_w_up, "moe_w_down": moe_w_down}


def reference(x, c, ctx, c_ctx, mod_w, mod_b, norm_g, attn_w_qkv, attn_w_o, attn_sink,
              dn_w_in, dn_conv, dn_a_log, dn_dt_bias, dn_norm_g, dn_w_o,
              ffn_w_gate, ffn_w_up, ffn_w_down, moe_router, moe_w_gate, moe_w_up, moe_w_down):
    B, S, D = x.shape
    Lc = ctx.shape[1]
    x_lat, x_ctx = x, ctx
    sc = jax.nn.silu(c)
    sc_ctx = jax.nn.silu(c_ctx)[None]
    for i in range(DEPTH):
        need_ctx = i < DEPTH - 1
        j = i // 2
        m_l = jnp.split(sc @ mod_w[i] + mod_b[i], N_MOD, axis=-1)
        m_c = jnp.split(sc_ctx @ mod_w[i] + mod_b[i], N_MOD, axis=-1)
        g_pre_mix, g_post_mix, g_pre_ffn, g_post_ffn = norm_g[i]

        h_l = _modulate(_rmsnorm(x_lat, g_pre_mix), m_l[0], m_l[1])
        h_c = _modulate(_rmsnorm(x_ctx, g_pre_mix), m_c[0], m_c[1])
        if i % 2 == 0:
            y_l, y_c = _window_gqa(h_l, h_c, attn_w_qkv[j], attn_w_o[j], attn_sink[j], need_ctx)
        else:
            y_l, y_c = _gated_deltanet(h_l, h_c, dn_w_in[j], dn_conv[j], dn_a_log[j], dn_dt_bias[j],
                                       dn_norm_g[j], dn_w_o[j], need_ctx)
        x_lat = x_lat + m_l[2][:, None] * _rmsnorm(y_l, g_post_mix)
        if need_ctx:
            x_ctx = x_ctx + m_c[2][:, None] * _rmsnorm(y_c, g_post_mix)

        h_all = _modulate(_rmsnorm(x_lat, g_pre_ffn), m_l[3], m_l[4]).reshape(B * S, D)
        if need_ctx:
            h_cf = _modulate(_rmsnorm(x_ctx, g_pre_ffn), m_c[3], m_c[4]).reshape(B * Lc, D)
            h_all = jnp.concatenate([h_all, h_cf], axis=0)
        if i % 2 == 0:
            y_all = _swiglu(h_all, ffn_w_gate[j], ffn_w_up[j], ffn_w_down[j])
        else:
            y_all = _moe_swiglu(h_all, moe_router[j], moe_w_gate[j], moe_w_up[j], moe_w_down[j])
        x_lat = x_lat + m_l[5][:, None] * _rmsnorm(y_all[:B * S].reshape(B, S, D), g_post_ffn)
        if need_ctx:
            x_ctx = x_ctx + m_c[5][:, None] * _rmsnorm(y_all[B * S:].reshape(B, Lc, D), g_post_ffn)
    return x_lat
```

```python
import functools
import math

import jax
import jax.numpy as jnp
from jax import lax
from jax.experimental import pallas as pl
from jax.experimental.pallas import tpu as pltpu

F32 = jnp.float32
BF16 = jnp.bfloat16

EPS = 1e-6
N_MOD = 6
HEAD_DIM = 128
ATTN_BLOCK = 128
GRID_W = 64
ROPE_BASE = 10000.0
ROPE_FREQS = HEAD_DIM // 4
DN_DK = 128
DN_DV = 128
CHUNK = 64
SHORT_CONV = 5
TOP_K = 2
NEG = -1e30

V7X_VMEM_LIMIT = 56 * 1024 * 1024


def _cparams(sem):
    return pltpu.CompilerParams(dimension_semantics=sem, vmem_limit_bytes=V7X_VMEM_LIMIT)


def _pick(n, prefs):
    for p in prefs:
        if n % p == 0:
            return p
    return n


def _silu(x):
    return x * (1.0 / (1.0 + jnp.exp(-x)))


def _dot(a, b):
    return jnp.dot(a, b, preferred_element_type=F32)


def _dot_hi(a, b):
    return jnp.dot(a, b, preferred_element_type=F32, precision=lax.Precision.HIGHEST)


def _dot_nt(a, b):
    return lax.dot_general(a, b, (((1,), (1,)), ((), ())), preferred_element_type=F32)


def _dot_tn(a, b):
    return lax.dot_general(a, b, (((0,), (0,)), ((), ())), preferred_element_type=F32)


def _mods_kernel(c_ref, w_ref, b_ref, o_ref):
    sc = _silu(c_ref[...]).astype(BF16)
    o_ref[0] = _dot(sc, w_ref[0].astype(BF16)) + b_ref[0]


def _modulation(cvec, mod_w, mod_b):
    depth, d, n = mod_w.shape
    r = cvec.shape[0]
    tn = _pick(n, (1024, 512, 256, 128))
    return pl.pallas_call(
        _mods_kernel,
        out_shape=jax.ShapeDtypeStruct((depth, r, n), F32),
        grid=(depth, n // tn),
        in_specs=[pl.BlockSpec((r, d), lambda l, j: (0, 0)),
                  pl.BlockSpec((1, d, tn), lambda l, j: (l, 0, j)),
                  pl.BlockSpec((1, 1, tn), lambda l, j: (l, 0, j))],
        out_specs=pl.BlockSpec((1, r, tn), lambda l, j: (l, 0, j)),
        compiler_params=_cparams(("parallel", "parallel")),
        name="modulation",
    )(cvec, mod_w, mod_b.reshape(depth, 1, n))


class _Layout:
    def __init__(self, batch, seq, ctx_len, d, seg_rows):
        self.batch, self.seq, self.ctx_len, self.d = batch, seq, ctx_len, d
        self.n_lat = batch * seq
        self.n_ctx = batch * ctx_len
        self.n_all = self.n_lat + self.n_ctx
        self.seg_rows = seg_rows

    def mod_index(self, layer, k, tm):
        batch, seq, seg_rows = self.batch, self.seq, self.seg_rows

        def index(i, *_):
            seg = jnp.minimum((i * tm) // seq, batch)
            return ((layer * seg_rows + seg) * N_MOD + k, 0, 0)
        return index

    def row_tile(self, prefs=(512, 256, 128, 64, 32, 16, 8)):
        for p in prefs:
            if self.seq % p == 0 and self.n_ctx % p == 0:
                return p
        raise ValueError("no row tile divides both the latent sequence and the context rows")


def _norm_mod_value(x, g, shift, scale):
    y = x * lax.rsqrt(jnp.mean(x * x, axis=-1, keepdims=True) + EPS)
    return (y * g) * (1.0 + scale) + shift


def _norm_mod_kernel(x_ref, g_ref, sh_ref, sc_ref, o_ref):
    o_ref[...] = _norm_mod_value(x_ref[...], g_ref[0], sh_ref[0], sc_ref[0]).astype(o_ref.dtype)


def _router_gates(h, router, n_e):
    logits = _dot_hi(h, router)
    lane = lax.broadcasted_iota(jnp.int32, logits.shape, 1)
    logits = jnp.where(lane < n_e, logits, -jnp.inf)
    m1 = jnp.max(logits, axis=-1, keepdims=True)
    i1 = jnp.min(jnp.where(logits == m1, lane, n_e), axis=-1, keepdims=True)
    rest = jnp.where(lane == i1, -jnp.inf, logits)
    m2 = jnp.max(rest, axis=-1, keepdims=True)
    i2 = jnp.min(jnp.where(rest == m2, lane, n_e), axis=-1, keepdims=True)
    e2 = jnp.exp(m2 - m1)
    w1 = 1.0 / (1.0 + e2)
    w2 = e2 / (1.0 + e2)
    return jnp.where(lane == i1, w1, 0.0) + jnp.where(lane == i2, w2, 0.0)


def _norm_mod_router_kernel(x_ref, g_ref, sh_ref, sc_ref, r_ref, o_ref, gate_ref, *, n_e):
    h = _norm_mod_value(x_ref[...], g_ref[0], sh_ref[0], sc_ref[0])
    o_ref[...] = h.astype(o_ref.dtype)
    gate_ref[...] = _router_gates(h, r_ref[...], n_e)


def _norm_mod(lay, x, mods3, norm3, layer, which, k_shift, k_scale, rows, router=None):
    d = lay.d
    tm = lay.row_tile((256, 128, 64, 32, 16, 8))
    in_specs = [pl.BlockSpec((tm, d), lambda i: (i, 0)),
                pl.BlockSpec((1, 1, d), lambda i: (layer * 4 + which, 0, 0)),
                pl.BlockSpec((1, 1, d), lay.mod_index(layer, k_shift, tm)),
                pl.BlockSpec((1, 1, d), lay.mod_index(layer, k_scale, tm))]
    if router is None:
        return pl.pallas_call(
            _norm_mod_kernel,
            out_shape=jax.ShapeDtypeStruct((rows, d), BF16),
            grid=(rows // tm,), in_specs=in_specs,
            out_specs=pl.BlockSpec((tm, d), lambda i: (i, 0)),
            compiler_params=_cparams(("parallel",)),
            name="norm_mod",
        )(x, norm3, mods3, mods3)
    n_e = router.shape[1]
    lanes = -(-n_e // HEAD_DIM) * HEAD_DIM
    router = jnp.pad(router, ((0, 0), (0, lanes - n_e)))
    return pl.pallas_call(
        functools.partial(_norm_mod_router_kernel, n_e=n_e),
        out_shape=(jax.ShapeDtypeStruct((rows, d), BF16), jax.ShapeDtypeStruct((rows, lanes), F32)),
        grid=(rows // tm,),
        in_specs=in_specs + [pl.BlockSpec((d, lanes), lambda i: (0, 0))],
        out_specs=(pl.BlockSpec((tm, d), lambda i: (i, 0)), pl.BlockSpec((tm, lanes), lambda i: (i, 0))),
        compiler_params=_cparams(("parallel",)),
        name="norm_mod_router",
    )(x, norm3, mods3, mods3, router)


def _proj_kernel(h_ref, w_ref, o_ref):
    o_ref[...] = _dot(h_ref[...], w_ref[...].astype(BF16)).astype(o_ref.dtype)


def _project(h, w, col0, n_cols, out_dtype):
    rows, d = h.shape
    tm = _pick(rows, (1024, 512, 256, 128, 64, 32, 16, 8))
    tn = _pick(math.gcd(n_cols, col0) if col0 else n_cols, (512, 256, 128))
    off = col0 // tn
    return pl.pallas_call(
        _proj_kernel,
        out_shape=jax.ShapeDtypeStruct((rows, n_cols), out_dtype),
        grid=(rows // tm, n_cols // tn),
        in_specs=[pl.BlockSpec((tm, d), lambda i, j: (i, 0)),
                  pl.BlockSpec((d, tn), lambda i, j: (0, j + off))],
        out_specs=pl.BlockSpec((tm, tn), lambda i, j: (i, j)),
        compiler_params=_cparams(("parallel", "parallel")),
        name="project",
    )(h, w)


def _rope_tables(n_tokens):
    rows_n = n_tokens // GRID_W
    row = jnp.repeat(jnp.arange(rows_n, dtype=F32), GRID_W)
    col = jnp.tile(jnp.arange(GRID_W, dtype=F32), rows_n)
    inv = ROPE_BASE ** (-jnp.arange(ROPE_FREQS, dtype=F32) / ROPE_FREQS)
    ar, ac = row[:, None] * inv, col[:, None] * inv
    cos = jnp.concatenate([jnp.cos(ar), jnp.cos(ar), jnp.cos(ac), jnp.cos(ac)], axis=1)
    sin = jnp.concatenate([-jnp.sin(ar), jnp.sin(ar), -jnp.sin(ac), jnp.sin(ac)], axis=1)
    return cos, sin


def _rope(x, cos, sin):
    lane = lax.broadcasted_iota(jnp.int32, x.shape, 1)
    swapped = jnp.where((lane % (2 * ROPE_FREQS)) < ROPE_FREQS,
                        pltpu.roll(x, HEAD_DIM - ROPE_FREQS, 1), pltpu.roll(x, ROPE_FREQS, 1))
    return x * cos + swapped * sin


def _attn_lat_kernel(sink_ref, q_ref, kp_ref, kc_ref, kn_ref, vp_ref, vc_ref, vn_ref, kx_ref, vx_ref,
                     cq_ref, sq_ref, cp_ref, sp_ref, cn_ref, sn_ref, o_ref, *, kv_heads, group, nb):
    qb = pl.program_id(1)
    scale = HEAD_DIM ** -0.5
    blk = ATTN_BLOCK
    rows = group * blk
    ri = lax.broadcasted_iota(jnp.int32, (rows, blk), 0) % blk
    ci = lax.broadcasted_iota(jnp.int32, (rows, blk), 1)
    ok_prev = jnp.logical_and(ci >= ri, qb > 0)
    ok_next = jnp.logical_and(ci <= ri, qb < nb - 1)
    cq, sq = cq_ref[...], sq_ref[...]
    for kh in range(kv_heads):
        ks = slice(kh * HEAD_DIM, (kh + 1) * HEAD_DIM)
        k_p = _rope(kp_ref[:, ks], cp_ref[...], sp_ref[...]).astype(BF16)
        k_c = _rope(kc_ref[:, ks], cq, sq).astype(BF16)
        k_n = _rope(kn_ref[:, ks], cn_ref[...], sn_ref[...]).astype(BF16)
        k_x = kx_ref[:, ks].astype(BF16)
        qs, sinks = [], []
        for g in range(group):
            hd = kh * group + g
            qs.append(_rope(q_ref[:, hd * HEAD_DIM:(hd + 1) * HEAD_DIM] * scale, cq, sq))
            sinks.append(jnp.full((blk, 1), sink_ref[hd], F32))
        q4 = jnp.concatenate(qs, axis=0).astype(BF16)
        sink = jnp.concatenate(sinks, axis=0)
        s_p = jnp.where(ok_prev, _dot_nt(q4, k_p), NEG)
        s_c = _dot_nt(q4, k_c)
        s_n = jnp.where(ok_next, _dot_nt(q4, k_n), NEG)
        s_x = _dot_nt(q4, k_x)
        m = jnp.maximum(jnp.maximum(jnp.max(s_p, -1, keepdims=True), jnp.max(s_c, -1, keepdims=True)),
                        jnp.maximum(jnp.max(s_n, -1, keepdims=True), jnp.max(s_x, -1, keepdims=True)))
        m = jnp.maximum(m, sink)
        p_p, p_c, p_n, p_x = jnp.exp(s_p - m), jnp.exp(s_c - m), jnp.exp(s_n - m), jnp.exp(s_x - m)
        den = (jnp.sum(p_p, -1, keepdims=True) + jnp.sum(p_c, -1, keepdims=True)
               + jnp.sum(p_n, -1, keepdims=True) + jnp.sum(p_x, -1, keepdims=True) + jnp.exp(sink - m))
        o4 = (_dot(p_p.astype(BF16), vp_ref[:, ks].astype(BF16)) + _dot(p_c.astype(BF16), vc_ref[:, ks].astype(BF16))
              + _dot(p_n.astype(BF16), vn_ref[:, ks].astype(BF16)) + _dot(p_x.astype(BF16), vx_ref[:, ks].astype(BF16)))
        o4 = o4 / den
        for g in range(group):
            hd = kh * group + g
            o_ref[:, hd * HEAD_DIM:(hd + 1) * HEAD_DIM] = o4[g * blk:(g + 1) * blk].astype(o_ref.dtype)


def _attn_ctx_kernel(sink_ref, q_ref, kx_ref, vx_ref, o_ref, *, kv_heads, group):
    scale = HEAD_DIM ** -0.5
    blk = q_ref.shape[0]
    for kh in range(kv_heads):
        ks = slice(kh * HEAD_DIM, (kh + 1) * HEAD_DIM)
        k_x = kx_ref[:, ks].astype(BF16)
        qs, sinks = [], []
        for g in range(group):
            hd = kh * group + g
            qs.append(q_ref[:, hd * HEAD_DIM:(hd + 1) * HEAD_DIM] * scale)
            sinks.append(jnp.full((blk, 1), sink_ref[hd], F32))
        q4 = jnp.concatenate(qs, axis=0).astype(BF16)
        sink = jnp.concatenate(sinks, axis=0)
        s_x = _dot_nt(q4, k_x)
        m = jnp.maximum(jnp.max(s_x, -1, keepdims=True), sink)
        p_x = jnp.exp(s_x - m)
        den = jnp.sum(p_x, -1, keepdims=True) + jnp.exp(sink - m)
        o4 = _dot(p_x.astype(BF16), vx_ref[:, ks].astype(BF16)) / den
        for g in range(group):
            hd = kh * group + g
            o_ref[:, hd * HEAD_DIM:(hd + 1) * HEAD_DIM] = o4[g * blk:(g + 1) * blk].astype(o_ref.dtype)


def _attention(lay, qkv, sink, n_heads, kv_heads, need_ctx):
    batch, seq, lc = lay.batch, lay.seq, lay.ctx_len
    group = n_heads // kv_heads
    a_q, a_kv = n_heads * HEAD_DIM, kv_heads * HEAD_DIM
    blk = ATTN_BLOCK
    nb = seq // blk
    qcol = a_q // a_kv
    ctx_blk0 = lay.n_lat // lc
    cos, sin = _rope_tables(seq)

    def kmap(shift, col):
        return lambda b, i, s: (b * nb + jnp.clip(i + shift, 0, nb - 1), col)

    def tmap(shift):
        return lambda b, i, s: (jnp.clip(i + shift, 0, nb - 1), 0)

    kspec = lambda shift, col: pl.BlockSpec((blk, a_kv), kmap(shift, col))
    tspec = lambda shift: pl.BlockSpec((blk, HEAD_DIM), tmap(shift))
    xspec = lambda col: pl.BlockSpec((lc, a_kv), lambda b, i, s: (ctx_blk0 + b, col))
    rows_out = lay.n_all if need_ctx else lay.n_lat
    out_lat = pl.pallas_call(
        functools.partial(_attn_lat_kernel, kv_heads=kv_heads, group=group, nb=nb),
        out_shape=jax.ShapeDtypeStruct((lay.n_lat, a_q), BF16),
        grid_spec=pltpu.PrefetchScalarGridSpec(
            num_scalar_prefetch=1, grid=(batch, nb),
            in_specs=[pl.BlockSpec((blk, a_q), lambda b, i, s: (b * nb + i, 0)),
                      kspec(-1, qcol), kspec(0, qcol), kspec(1, qcol),
                      kspec(-1, qcol + 1), kspec(0, qcol + 1), kspec(1, qcol + 1),
                      xspec(qcol), xspec(qcol + 1),
                      tspec(0), tspec(0), tspec(-1), tspec(-1), tspec(1), tspec(1)],
            out_specs=pl.BlockSpec((blk, a_q), lambda b, i, s: (b * nb + i, 0))),
        compiler_params=_cparams(("parallel", "parallel")),
        name="attn_latent",
    )(sink, qkv, qkv, qkv, qkv, qkv, qkv, qkv, qkv, qkv, cos, sin, cos, sin, cos, sin)
    if not need_ctx:
        return out_lat
    cb = _pick(lc, (128, 64, 32, 16, 8))
    ncb = lc // cb
    row0 = lay.n_lat // cb
    out_ctx = pl.pallas_call(
        functools.partial(_attn_ctx_kernel, kv_heads=kv_heads, group=group),
        out_shape=jax.ShapeDtypeStruct((lay.n_ctx, a_q), BF16),
        grid_spec=pltpu.PrefetchScalarGridSpec(
            num_scalar_prefetch=1, grid=(batch, ncb),
            in_specs=[pl.BlockSpec((cb, a_q), lambda b, i, s: (row0 + b * ncb + i, 0)),
                      xspec(qcol), xspec(qcol + 1)],
            out_specs=pl.BlockSpec((cb, a_q), lambda b, i, s: (b * ncb + i, 0))),
        compiler_params=_cparams(("parallel", "parallel")),
        name="attn_context",
    )(sink, qkv, qkv, qkv)
    del rows_out
    return jnp.concatenate([out_lat, out_ctx], axis=0)


def _post_value(y, x, g, gate):
    r = y * lax.rsqrt(jnp.mean(y * y, axis=-1, keepdims=True) + EPS)
    return x + gate * (r * g)


def _out_proj_kernel(a_ref, w_ref, x_ref, g_ref, gate_ref, o_ref, acc_ref):
    k = pl.program_id(1)

    @pl.when(k == 0)
    def _():
        acc_ref[...] = jnp.zeros_like(acc_ref)

    acc_ref[...] += _dot(a_ref[...], w_ref[...].astype(BF16))

    @pl.when(k == pl.num_programs(1) - 1)
    def _():
        o_ref[...] = _post_value(acc_ref[...], x_ref[...], g_ref[0], gate_ref[0])


def _gdn_out_proj_kernel(of_ref, ob_ref, z_ref, nw_ref, w_ref, x_ref, g_ref, gate_ref, o_ref, acc_ref):
    k = pl.program_id(1)

    @pl.when(k == 0)
    def _():
        acc_ref[...] = jnp.zeros_like(acc_ref)

    o = of_ref[...] + ob_ref[...]
    z = z_ref[...]
    nw = nw_ref[...]
    parts = []
    for h in range(o.shape[1] // DN_DV):
        sl = slice(h * DN_DV, (h + 1) * DN_DV)
        oh = o[:, sl]
        y = oh * lax.rsqrt(jnp.mean(oh * oh, axis=-1, keepdims=True) + EPS) * nw
        parts.append((y * _silu(z[:, sl])).astype(BF16))
    a = jnp.concatenate(parts, axis=1) if len(parts) > 1 else parts[0]
    acc_ref[...] += _dot(a, w_ref[...].astype(BF16))

    @pl.when(k == pl.num_programs(1) - 1)
    def _():
        o_ref[...] = _post_value(acc_ref[...], x_ref[...], g_ref[0], gate_ref[0])


def _out_proj(lay, a, w, x, mods3, norm3, layer, which, k_gate, rows):
    kdim, d = w.shape
    tm = lay.row_tile((512, 256, 128, 64, 32, 16, 8))
    tk = _pick(kdim, (512, 256, 128))
    return pl.pallas_call(
        _out_proj_kernel,
        out_shape=jax.ShapeDtypeStruct((rows, d), F32),
        grid=(rows // tm, kdim // tk),
        in_specs=[pl.BlockSpec((tm, tk), lambda i, k: (i, k)),
                  pl.BlockSpec((tk, d), lambda i, k: (k, 0)),
                  pl.BlockSpec((tm, d), lambda i, k: (i, 0)),
                  pl.BlockSpec((1, 1, d), lambda i, k: (layer * 4 + which, 0, 0)),
                  pl.BlockSpec((1, 1, d), lay.mod_index(layer, k_gate, tm))],
        out_specs=pl.BlockSpec((tm, d), lambda i, k: (i, 0)),
        scratch_shapes=[pltpu.VMEM((tm, d), F32)],
        compiler_params=_cparams(("parallel", "arbitrary")),
        name="out_proj",
    )(a, w, x, norm3, mods3)


def _gdn_out_proj(lay, o_f, o_b, p_all, z_col0, norm_w, w, x, mods3, norm3, layer, which, k_gate, rows):
    kdim, d = w.shape
    tm = lay.row_tile((512, 256, 128, 64, 32, 16, 8))
    tk = _pick(math.gcd(kdim, z_col0), (512, 256, 128))
    zoff = z_col0 // tk
    return pl.pallas_call(
        _gdn_out_proj_kernel,
        out_shape=jax.ShapeDtypeStruct((rows, d), F32),
        grid=(rows // tm, kdim // tk),
        in_specs=[pl.BlockSpec((tm, tk), lambda i, k: (i, k)),
                  pl.BlockSpec((tm, tk), lambda i, k: (i, k)),
                  pl.BlockSpec((tm, tk), lambda i, k: (i, k + zoff)),
                  pl.BlockSpec((1, DN_DV), lambda i, k: (0, 0)),
                  pl.BlockSpec((tk, d), lambda i, k: (k, 0)),
                  pl.BlockSpec((tm, d), lambda i, k: (i, 0)),
                  pl.BlockSpec((1, 1, d), lambda i, k: (layer * 4 + which, 0, 0)),
                  pl.BlockSpec((1, 1, d), lay.mod_index(layer, k_gate, tm))],
        out_specs=pl.BlockSpec((tm, d), lambda i, k: (i, 0)),
        scratch_shapes=[pltpu.VMEM((tm, d), F32)],
        compiler_params=_cparams(("parallel", "arbitrary")),
        name="gdn_out_proj",
    )(o_f, o_b, p_all, norm_w.reshape(1, DN_DV), w, x, norm3, mods3)


def _ffn_kernel(h_ref, gates_ref, wg_ref, wu_ref, wd_ref, x_ref, g_ref, gate_ref, o_ref, acc_ref, *, use_gates):
    e, f = pl.program_id(1), pl.program_id(2)

    @pl.when(jnp.logical_and(e == 0, f == 0))
    def _():
        acc_ref[...] = jnp.zeros_like(acc_ref)

    h = h_ref[...]
    gt = _dot(h, wg_ref[0].astype(BF16))
    up = _dot(h, wu_ref[0].astype(BF16))
    a = _silu(gt) * up
    if use_gates:
        gates = gates_ref[...]
        lane = lax.broadcasted_iota(jnp.int32, gates.shape, 1)
        a = a * jnp.sum(jnp.where(lane == e, gates, 0.0), axis=-1, keepdims=True)
    acc_ref[...] += _dot(a.astype(BF16), wd_ref[0].astype(BF16))

    @pl.when(jnp.logical_and(e == pl.num_programs(1) - 1, f == pl.num_programs(2) - 1))
    def _():
        o_ref[...] = _post_value(acc_ref[...], x_ref[...], g_ref[0], gate_ref[0])


def _ffn(lay, h, gates, wg, wu, wd, x, mods3, norm3, layer, which, k_gate, rows):
    n_e, d, dff = wg.shape
    tm = lay.row_tile((512, 256, 128, 64, 32, 16, 8))
    tf = _pick(dff, (256, 128))
    use_gates = gates is not None
    if gates is None:
        gates = jnp.ones((rows, 1), F32)
    ge = gates.shape[1]
    return pl.pallas_call(
        functools.partial(_ffn_kernel, use_gates=use_gates),
        out_shape=jax.ShapeDtypeStruct((rows, d), F32),
        grid=(rows // tm, n_e, dff // tf),
        in_specs=[pl.BlockSpec((tm, d), lambda i, e, f: (i, 0)),
                  pl.BlockSpec((tm, ge), lambda i, e, f: (i, 0)),
                  pl.BlockSpec((1, d, tf), lambda i, e, f: (e, 0, f)),
                  pl.BlockSpec((1, d, tf), lambda i, e, f: (e, 0, f)),
                  pl.BlockSpec((1, tf, d), lambda i, e, f: (e, f, 0)),
                  pl.BlockSpec((tm, d), lambda i, e, f: (i, 0)),
                  pl.BlockSpec((1, 1, d), lambda i, e, f: (layer * 4 + which, 0, 0)),
                  pl.BlockSpec((1, 1, d), lay.mod_index(layer, k_gate, tm))],
        out_specs=pl.BlockSpec((tm, d), lambda i, e, f: (i, 0)),
        scratch_shapes=[pltpu.VMEM((tm, d), F32)],
        compiler_params=_cparams(("parallel", "arbitrary", "arbitrary")),
        name="ffn",
    )(h, gates, wg, wu, wd, x, norm3, mods3)


def _gdn_conv_kernel(cur_ref, prev_ref, next_ref, w_ref, o_ref, *, lay, tr, tc, qk_cols, q_cols):
    i, j = pl.program_id(0), pl.program_id(1)
    r0 = i * tr
    in_lat = r0 < lay.n_lat
    pos = jnp.where(in_lat, r0 % lay.seq, (r0 - lay.n_lat) % lay.ctx_len)
    seg = jnp.where(in_lat, lay.seq, lay.ctx_len)
    has_prev = (pos > 0).astype(F32)
    has_next = (pos + tr < seg).astype(F32)
    x = cur_ref[...]
    prev = prev_ref[...] * has_prev
    nxt = next_ref[...] * has_next
    w = w_ref[...]
    row = lax.broadcasted_iota(jnp.int32, x.shape, 0)
    xm2 = jnp.where(row == 0, prev[6:7], jnp.where(row == 1, prev[7:8], pltpu.roll(x, 2, 0)))
    xm1 = jnp.where(row == 0, prev[7:8], pltpu.roll(x, 1, 0))
    xp1 = jnp.where(row == tr - 1, nxt[0:1], pltpu.roll(x, tr - 1, 0))
    xp2 = jnp.where(row == tr - 2, nxt[0:1], jnp.where(row == tr - 1, nxt[1:2], pltpu.roll(x, tr - 2, 0)))
    y = w[0:1] * xm2 + w[1:2] * xm1 + w[2:3] * x + w[3:4] * xp1 + w[4:5] * xp2
    y = _silu(y)
    c0 = j * tc

    @pl.when(c0 >= qk_cols)
    def _():
        o_ref[...] = y

    @pl.when(c0 < qk_cols)
    def _():
        qscale = jnp.where(c0 < q_cols, DN_DK ** -0.5, 1.0).astype(F32)
        for hh in range(tc // DN_DK):
            sl = slice(hh * DN_DK, (hh + 1) * DN_DK)
            yh = y[:, sl]
            o_ref[:, sl] = yh * (lax.rsqrt(jnp.sum(yh * yh, axis=-1, keepdims=True) + EPS) * qscale)


def _gdn_conv(lay, p_all, conv_w, n_qkv, n_k):
    tr = _pick(math.gcd(lay.seq, lay.ctx_len), (256, 128, 64, 32, 16, 8))
    tc = _pick(math.gcd(n_qkv, n_k), (512, 256, 128))
    hb = tr // 8
    n8 = lay.n_all // 8
    return pl.pallas_call(
        functools.partial(_gdn_conv_kernel, lay=lay, tr=tr, tc=tc, qk_cols=2 * n_k, q_cols=n_k),
        out_shape=jax.ShapeDtypeStruct((lay.n_all, n_qkv), F32),
        grid=(lay.n_all // tr, n_qkv // tc),
        in_specs=[pl.BlockSpec((tr, tc), lambda i, j: (i, j)),
                  pl.BlockSpec((8, tc), lambda i, j: (jnp.maximum(i * hb - 1, 0), j)),
                  pl.BlockSpec((8, tc), lambda i, j: (jnp.minimum((i + 1) * hb, n8 - 1), j)),
                  pl.BlockSpec((SHORT_CONV, tc), lambda i, j: (0, j))],
        out_specs=pl.BlockSpec((tr, tc), lambda i, j: (i, j)),
        compiler_params=_cparams(("parallel", "parallel")),
        name="gdn_conv",
    )(p_all, p_all, p_all, conv_w)


def _softplus(x):
    return jnp.maximum(x, 0.0) + jnp.log1p(jnp.exp(-jnp.abs(x)))


def _gdn_gate_kernel(ab_ref, a_ref, dt_ref, o_ref, *, hv):
    n, width = ab_ref.shape
    lane = lax.broadcasted_iota(jnp.int32, (CHUNK, width), 1)
    is_g = (lane // hv) % 2 == 0
    is_bwd = lane >= 2 * hv
    ri = lax.broadcasted_iota(jnp.int32, (CHUNK, CHUNK), 0)
    ci = lax.broadcasted_iota(jnp.int32, (CHUNK, CHUNK), 1)
    lower = (ri >= ci).astype(F32)
    upper = (ri <= ci).astype(F32)
    for c in range(n // CHUNK):
        rs = slice(c * CHUNK, (c + 1) * CHUNK)
        x = ab_ref[rs, :]
        g = -a_ref[...] * _softplus(x + dt_ref[...])
        beta = 1.0 / (1.0 + jnp.exp(-x))
        cum = jnp.where(is_bwd, _dot_hi(upper, g), _dot_hi(lower, g))
        o_ref[rs, :] = jnp.where(is_g, cum, beta)


def _gdn_gates(lay, ab, a_log, dt_bias, hv):
    tr = lay.row_tile((256, 128, 64))
    width = 4 * hv
    zeros = jnp.zeros((hv,), F32)
    a_row = jnp.concatenate([jnp.exp(a_log[0].astype(F32)), zeros, jnp.exp(a_log[1].astype(F32)), zeros])
    dt_row = jnp.concatenate([dt_bias[0].astype(F32), zeros, dt_bias[1].astype(F32), zeros])
    return pl.pallas_call(
        functools.partial(_gdn_gate_kernel, hv=hv),
        out_shape=jax.ShapeDtypeStruct((lay.n_all, width), F32),
        grid=(lay.n_all // tr,),
        in_specs=[pl.BlockSpec((tr, width), lambda i: (i, 0)),
                  pl.BlockSpec((1, width), lambda i: (0, 0)),
                  pl.BlockSpec((1, width), lambda i: (0, 0))],
        out_specs=pl.BlockSpec((tr, width), lambda i: (i, 0)),
        compiler_params=_cparams(("parallel",)),
        name="gdn_gates",
    )(ab, a_row.reshape(1, width), dt_row.reshape(1, width))


def _gdn_chunk_kernel(q_ref, k_ref, v_ref, gcol_ref, grow_ref, o_ref, s_ref, *, reverse, n_chunks):
    step = pl.program_id(2)

    @pl.when(step == 0)
    def _():
        s_ref[...] = jnp.zeros_like(s_ref)

    c = CHUNK
    lane = lax.broadcasted_iota(jnp.int32, (c, 2 * c), 1)
    left = lane < c
    ti = lax.broadcasted_iota(jnp.int32, (c, 2 * c), 0)
    tj = lane % c
    if reverse:
        incl, strict = ti <= tj, ti < tj
    else:
        incl, strict = ti >= tj, ti > tj
    eye = (ti == tj).astype(F32)
    left2 = lax.broadcasted_iota(jnp.int32, (2 * c, 2 * c), 1) < c
    top2 = lax.broadcasted_iota(jnp.int32, (2 * c, 2 * c), 0) < c
    diag_blocks = left2 == top2

    def blockdiag(x):
        return jnp.where(diag_blocks, jnp.concatenate([x, x], axis=0), 0.0)

    def side(col_a, col_b):
        return jnp.where(left, col_a, col_b)

    order = range(n_chunks - 1, -1, -1) if reverse else range(n_chunks)
    last = 0 if reverse else c - 1
    for ch in order:
        rs = slice(ch * c, (ch + 1) * c)
        k = k_ref[rs, :]
        q = q_ref[rs, :]
        gcol = gcol_ref[0, 0, rs, :]
        g_a, g_b, b_a, b_b = gcol[:, 0:1], gcol[:, 1:2], gcol[:, 2:3], gcol[:, 3:4]
        grow = jnp.concatenate([grow_ref[0, 0, 0:1, rs], grow_ref[0, 0, 1:2, rs]], axis=1)
        diff = side(g_a, g_b) - grow
        decay = jnp.where(incl, jnp.exp(jnp.minimum(diff, 0.0)), 0.0)
        kb16 = k.astype(BF16)
        kq = _dot_nt(jnp.concatenate([kb16, q.astype(BF16)], axis=0), jnp.concatenate([kb16, kb16], axis=0))
        m = jnp.where(strict, kq[:c] * decay * side(b_a, b_b), 0.0)
        qkd = kq[c:] * decay
        p = eye - m
        n_pow = _dot_hi(m, blockdiag(m))
        levels = int(math.log2(c)) - 1
        for lv in range(levels):
            bd = blockdiag(n_pow)
            if lv < levels - 1:
                both = _dot_hi(jnp.concatenate([n_pow, p], axis=0), bd)
                n_pow, p = both[:c], p + both[c:]
            else:
                p = p + _dot_hi(p, bd)
        eg_a, eg_b = jnp.exp(g_a), jnp.exp(g_b)
        rhs = jnp.concatenate(
            [jnp.concatenate([v_ref[rs, 0:DN_DV] * b_a, k * (b_a * eg_a)], axis=1),
             jnp.concatenate([v_ref[rs, DN_DV:2 * DN_DV] * b_b, k * (b_b * eg_b)], axis=1)], axis=0)
        sol = _dot_hi(blockdiag(p), rhs)
        for hd, (g_h, eg_h) in enumerate(((g_a, eg_a), (g_b, eg_b))):
            u = sol[hd * c:(hd + 1) * c, 0:DN_DV]
            w = sol[hd * c:(hd + 1) * c, DN_DV:]
            g_last = g_h[last:last + 1, :]
            s_old = s_ref[hd]
            ws = _dot(jnp.concatenate([w, q * eg_h], axis=0).astype(BF16), s_old.astype(BF16))
            v_new = u - ws[:c]
            vn16 = v_new.astype(BF16)
            intra = _dot(qkd[:, hd * c:(hd + 1) * c].astype(BF16), vn16)
            o_ref[rs, hd * DN_DV:(hd + 1) * DN_DV] = (ws[c:] + intra).astype(o_ref.dtype)
            kd = (k * jnp.exp(g_last - g_h)).astype(BF16)
            s_ref[hd] = s_old * jnp.exp(g_last) + _dot_tn(kd, vn16)


def _gdn_scan(lay, qkv, gcol, grow, hk, hv, reverse):
    rb = _pick(lay.ctx_len, (256, 128, 64))
    n_chunks = rb // CHUNK
    n_cb = lay.ctx_len // rb
    n_lb = lay.seq // rb
    steps = n_cb + n_lb
    ctx0 = lay.n_lat // rb
    rep = hv // hk
    assert rep == 2, "the scan kernel pairs the two value heads of each key head"

    def rowblock(b, s):
        if reverse:
            return jnp.where(s < n_cb, ctx0 + b * n_cb + (n_cb - 1 - s), b * n_lb + (n_lb - 1 - (s - n_cb)))
        return jnp.where(s < n_cb, ctx0 + b * n_cb + s, b * n_lb + (s - n_cb))

    return pl.pallas_call(
        functools.partial(_gdn_chunk_kernel, reverse=reverse, n_chunks=n_chunks),
        out_shape=jax.ShapeDtypeStruct((lay.n_all, hv * DN_DV), F32),
        grid=(lay.batch, hk, steps),
        in_specs=[pl.BlockSpec((rb, DN_DK), lambda b, p, s: (rowblock(b, s), p)),
                  pl.BlockSpec((rb, DN_DK), lambda b, p, s: (rowblock(b, s), hk + p)),
                  pl.BlockSpec((rb, 2 * DN_DV), lambda b, p, s: (rowblock(b, s), hk + p)),
                  pl.BlockSpec((1, 1, rb, 4), lambda b, p, s: (p, 0, rowblock(b, s), 0)),
                  pl.BlockSpec((1, 1, 8, rb), lambda b, p, s: (p, 0, 0, rowblock(b, s)))],
        out_specs=pl.BlockSpec((rb, 2 * DN_DV), lambda b, p, s: (rowblock(b, s), p)),
        scratch_shapes=[pltpu.VMEM((2, DN_DK, DN_DV), F32)],
        compiler_params=_cparams(("parallel", "parallel", "arbitrary")),
        name="gdn_scan_bwd" if reverse else "gdn_scan_fwd",
    )(qkv, qkv, qkv, gcol, grow)


def _gate_layouts(gates, hv, direction):
    n = gates.shape[0]
    g = gates[:, direction * 2 * hv: direction * 2 * hv + hv].reshape(n, hv // 2, 2)
    b = gates[:, direction * 2 * hv + hv: (direction + 1) * 2 * hv].reshape(n, hv // 2, 2)
    col = jnp.concatenate([g, b], axis=2).transpose(1, 0, 2)[:, None]
    row = jnp.pad(g.transpose(1, 2, 0), ((0, 0), (0, 6), (0, 0)))[:, None]
    return col, row


def kernel(x, c, ctx, c_ctx, mod_w, mod_b, norm_g, attn_w_qkv, attn_w_o, attn_sink, dn_w_in, dn_conv, dn_a_log,
           dn_dt_bias, dn_norm_g, dn_w_o, ffn_w_gate, ffn_w_up, ffn_w_down, moe_router, moe_w_gate, moe_w_up,
           moe_w_down):
    batch, seq, d = x.shape
    lc = ctx.shape[1]
    depth = mod_w.shape[0]
    seg_rows = -(-(batch + 1) // 8) * 8
    lay = _Layout(batch, seq, lc, d, seg_rows)

    n_heads = attn_sink.shape[1]
    kv_heads = (attn_w_qkv.shape[2] // HEAD_DIM - n_heads) // 2
    hv = dn_a_log.shape[2]
    hk = (dn_w_in.shape[2] - 4 * hv - 2 * hv * DN_DV) // (2 * DN_DK)
    n_k, n_v = hk * DN_DK, hv * DN_DV
    n_qkv = 2 * n_k + n_v

    cvec = jnp.concatenate([c, c_ctx[None], jnp.zeros((seg_rows - batch - 1, d), F32)], axis=0)
    mods3 = _modulation(cvec, mod_w, mod_b).reshape(depth * seg_rows * N_MOD, 1, d)
    norm3 = norm_g.reshape(depth * 4, 1, d)

    xs = jnp.concatenate([x.reshape(batch * seq, d), ctx.reshape(batch * lc, d)], axis=0)
    for i in range(depth):
        need_ctx = i < depth - 1
        rows = lay.n_all if need_ctx else lay.n_lat
        j = i // 2
        h = _norm_mod(lay, xs, mods3, norm3, i, 0, 0, 1, lay.n_all)
        if i % 2 == 0:
            qkv = _project(h, attn_w_qkv[j], 0, attn_w_qkv.shape[2], F32)
            a = _attention(lay, qkv, attn_sink[j], n_heads, kv_heads, need_ctx)
            xs = _out_proj(lay, a, attn_w_o[j], xs, mods3, norm3, i, 1, 2, rows)
        else:
            p_all = _project(h, dn_w_in[j], 0, n_qkv + n_v, F32)
            ab = _project(h, dn_w_in[j], n_qkv + n_v, 4 * hv, F32)
            qkv = _gdn_conv(lay, p_all, dn_conv[j], n_qkv, n_k)
            gates = _gdn_gates(lay, ab, dn_a_log[j], dn_dt_bias[j], hv)
            o_f = _gdn_scan(lay, qkv, *_gate_layouts(gates, hv, 0), hk, hv, False)
            o_b = _gdn_scan(lay, qkv, *_gate_layouts(gates, hv, 1), hk, hv, True)
            xs = _gdn_out_proj(lay, o_f, o_b, p_all, n_qkv, dn_norm_g[j], dn_w_o[j], xs, mods3, norm3,
                               i, 1, 2, rows)
        if i % 2 == 0:
            h = _norm_mod(lay, xs, mods3, norm3, i, 2, 3, 4, rows)
            xs = _ffn(lay, h, None, ffn_w_gate[j][None], ffn_w_up[j][None], ffn_w_down[j][None],
                      xs, mods3, norm3, i, 3, 5, rows)
        else:
            h, gts = _norm_mod(lay, xs, mods3, norm3, i, 2, 3, 4, rows, router=moe_router[j])
            xs = _ffn(lay, h, gts, moe_w_gate[j], moe_w_up[j], moe_w_down[j], xs, mods3, norm3, i, 3, 5, rows)
    return xs[:lay.n_lat].reshape(batch, seq, d)
```

```python
import functools
import math

import jax
import jax.numpy as jnp
from jax import lax
from jax.experimental import pallas as pl
from jax.experimental.pallas import tpu as pltpu

F32 = jnp.float32
BF16 = jnp.bfloat16

EPS = 1e-6
N_MOD = 6
LANES = 128
HEAD_DIM = 128
ATTN_BLOCK = 128
GRID_W = 64
ROPE_BASE = 10000.0
ROPE_FREQS = HEAD_DIM // 4
DN_DK = 128
DN_DV = 128
CHUNK = 64
SOLVE_BLOCK = 16
SHORT_CONV = 5
TOP_K = 2
NEG = -1e30
MOE_TILE = 1024

V7X_VMEM_LIMIT = 56 * 1024 * 1024


def _cparams(sem):
    return pltpu.CompilerParams(dimension_semantics=sem, vmem_limit_bytes=V7X_VMEM_LIMIT)


def _pick(n, prefs):
    for p in prefs:
        if n % p == 0:
            return p
    return n


def _silu(x):
    return x * (1.0 / (1.0 + jnp.exp(-x)))


def _dot(a, b):
    return jnp.dot(a, b, preferred_element_type=F32)


def _dot_hi(a, b):
    return jnp.dot(a, b, preferred_element_type=F32, precision=lax.Precision.HIGHEST)


def _dot_nt(a, b):
    return lax.dot_general(a, b, (((1,), (1,)), ((), ())), preferred_element_type=F32)


def _dot_tn(a, b):
    return lax.dot_general(a, b, (((0,), (0,)), ((), ())), preferred_element_type=F32)


def _split(a):
    hi = a.astype(BF16)
    return hi, (a - hi.astype(F32)).astype(BF16)


def _dot3(a, b_hi, b_lo):
    a_hi, a_lo = _split(a)
    return _dot(a_hi, b_hi) + (_dot(a_hi, b_lo) + _dot(a_lo, b_hi))


def _mods_kernel(c_ref, w_ref, b_ref, o_ref):
    sc = _silu(c_ref[...]).astype(BF16)
    o_ref[0] = _dot(sc, w_ref[0].astype(BF16)) + b_ref[0]


def _modulation(cvec, mod_w, mod_b):
    depth, d, n = mod_w.shape
    r = cvec.shape[0]
    tn = _pick(n, (1024, 512, 256, 128))
    return pl.pallas_call(
        _mods_kernel,
        out_shape=jax.ShapeDtypeStruct((depth, r, n), F32),
        grid=(depth, n // tn),
        in_specs=[pl.BlockSpec((r, d), lambda l, j: (0, 0)),
                  pl.BlockSpec((1, d, tn), lambda l, j: (l, 0, j)),
                  pl.BlockSpec((1, 1, tn), lambda l, j: (l, 0, j))],
        out_specs=pl.BlockSpec((1, r, tn), lambda l, j: (l, 0, j)),
        compiler_params=_cparams(("parallel", "parallel")),
        name="modulation",
    )(cvec, mod_w, mod_b.reshape(depth, 1, n))


class _Layout:
    def __init__(self, batch, seq, ctx_len, d, seg_rows):
        self.batch, self.seq, self.ctx_len, self.d = batch, seq, ctx_len, d
        self.n_lat = batch * seq
        self.n_ctx = batch * ctx_len
        self.n_all = self.n_lat + self.n_ctx
        self.seg_rows = seg_rows

    def mod_index(self, layer, k, tm):
        batch, seq, seg_rows = self.batch, self.seq, self.seg_rows

        def index(i, *_):
            seg = jnp.minimum((i * tm) // seq, batch)
            return ((layer * seg_rows + seg) * N_MOD + k, 0, 0)
        return index

    def row_tile(self, prefs=(512, 256, 128, 64, 32, 16, 8)):
        for p in prefs:
            if self.seq % p == 0 and self.n_ctx % p == 0:
                return p
        raise ValueError("no row tile divides both the latent sequence and the context rows")


def _norm_mod_value(x, g, shift, scale):
    y = x * lax.rsqrt(jnp.mean(x * x, axis=-1, keepdims=True) + EPS)
    return (y * g) * (1.0 + scale) + shift


def _norm_mod_kernel(x_ref, g_ref, sh_ref, sc_ref, o_ref):
    o_ref[...] = _norm_mod_value(x_ref[...], g_ref[0], sh_ref[0], sc_ref[0]).astype(o_ref.dtype)


def _route(h, router, n_e):
    logits = _dot_hi(h, router)
    lane = lax.broadcasted_iota(jnp.int32, logits.shape, 1)
    logits = jnp.where(lane < n_e, logits, -jnp.inf)
    m1 = jnp.max(logits, axis=-1, keepdims=True)
    i1 = jnp.min(jnp.where(logits == m1, lane, n_e), axis=-1, keepdims=True)
    rest = jnp.where(lane == i1, -jnp.inf, logits)
    m2 = jnp.max(rest, axis=-1, keepdims=True)
    i2 = jnp.min(jnp.where(rest == m2, lane, n_e), axis=-1, keepdims=True)
    e2 = jnp.exp(m2 - m1)
    w1 = 1.0 / (1.0 + e2)
    w2 = e2 / (1.0 + e2)
    out = jnp.where(lane == 0, i1.astype(F32), 0.0) + jnp.where(lane == 1, i2.astype(F32), 0.0)
    return out + jnp.where(lane == 2, w1, 0.0) + jnp.where(lane == 3, w2, 0.0)


def _norm_mod_router_kernel(x_ref, g_ref, sh_ref, sc_ref, r_ref, h3_ref, route_ref, *, n_e):
    h = _norm_mod_value(x_ref[...], g_ref[0], sh_ref[0], sc_ref[0])
    for s in range(h3_ref.shape[1]):
        h3_ref[:, s, :] = h[:, s * LANES:(s + 1) * LANES]
    route_ref[...] = _route(h, r_ref[...], n_e)


def _norm_mod(lay, x, mods3, norm3, layer, which, k_shift, k_scale, rows, router=None):
    d = lay.d
    tm = lay.row_tile((256, 128, 64, 32, 16, 8))
    in_specs = [pl.BlockSpec((tm, d), lambda i: (i, 0)),
                pl.BlockSpec((1, 1, d), lambda i: (layer * 4 + which, 0, 0)),
                pl.BlockSpec((1, 1, d), lay.mod_index(layer, k_shift, tm)),
                pl.BlockSpec((1, 1, d), lay.mod_index(layer, k_scale, tm))]
    if router is None:
        return pl.pallas_call(
            _norm_mod_kernel,
            out_shape=jax.ShapeDtypeStruct((rows, d), BF16),
            grid=(rows // tm,), in_specs=in_specs,
            out_specs=pl.BlockSpec((tm, d), lambda i: (i, 0)),
            compiler_params=_cparams(("parallel",)),
            name="norm_mod",
        )(x, norm3, mods3, mods3)
    n_e = router.shape[1]
    router = jnp.pad(router, ((0, 0), (0, LANES - n_e)))
    slabs = d // LANES
    return pl.pallas_call(
        functools.partial(_norm_mod_router_kernel, n_e=n_e),
        out_shape=(jax.ShapeDtypeStruct((rows, slabs, LANES), F32), jax.ShapeDtypeStruct((rows, LANES), F32)),
        grid=(rows // tm,),
        in_specs=in_specs + [pl.BlockSpec((d, LANES), lambda i: (0, 0))],
        out_specs=(pl.BlockSpec((tm, slabs, LANES), lambda i: (i, 0, 0)),
                   pl.BlockSpec((tm, LANES), lambda i: (i, 0))),
        compiler_params=_cparams(("parallel",)),
        name="norm_mod_router",
    )(x, norm3, mods3, mods3, router)


def _proj_kernel(h_ref, w_ref, o_ref):
    o_ref[...] = _dot(h_ref[...], w_ref[0].astype(BF16)).astype(o_ref.dtype)


def _project(h, w, wi, col0, n_cols, out_dtype):
    rows, d = h.shape
    tm = _pick(rows, (1024, 512, 256, 128, 64, 32, 16, 8))
    tn = _pick(math.gcd(n_cols, col0) if col0 else n_cols, (512, 256, 128))
    off = col0 // tn
    return pl.pallas_call(
        _proj_kernel,
        out_shape=jax.ShapeDtypeStruct((rows, n_cols), out_dtype),
        grid=(rows // tm, n_cols // tn),
        in_specs=[pl.BlockSpec((tm, d), lambda i, j: (i, 0)),
                  pl.BlockSpec((1, d, tn), lambda i, j: (wi, 0, j + off))],
        out_specs=pl.BlockSpec((tm, tn), lambda i, j: (i, j)),
        compiler_params=_cparams(("parallel", "parallel")),
        name="project",
    )(h, w)


def _rope_tables(n_tokens):
    rows_n = n_tokens // GRID_W
    row = jnp.repeat(jnp.arange(rows_n, dtype=F32), GRID_W)
    col = jnp.tile(jnp.arange(GRID_W, dtype=F32), rows_n)
    inv = ROPE_BASE ** (-jnp.arange(ROPE_FREQS, dtype=F32) / ROPE_FREQS)
    ar, ac = row[:, None] * inv, col[:, None] * inv
    cos = jnp.concatenate([jnp.cos(ar), jnp.cos(ar), jnp.cos(ac), jnp.cos(ac)], axis=1)
    sin = jnp.concatenate([-jnp.sin(ar), jnp.sin(ar), -jnp.sin(ac), jnp.sin(ac)], axis=1)
    return cos, sin


def _rope(x, cos, sin):
    lane = lax.broadcasted_iota(jnp.int32, x.shape, 1)
    swapped = jnp.where((lane % (2 * ROPE_FREQS)) < ROPE_FREQS,
                        pltpu.roll(x, HEAD_DIM - ROPE_FREQS, 1), pltpu.roll(x, ROPE_FREQS, 1))
    return x * cos + swapped * sin


def _attn_lat_kernel(sink_ref, q_ref, kp_ref, kc_ref, kn_ref, vp_ref, vc_ref, vn_ref, kx_ref, vx_ref,
                     cq_ref, sq_ref, cp_ref, sp_ref, cn_ref, sn_ref, o_ref, *, kv_heads, group, nb):
    qb = pl.program_id(1)
    scale = HEAD_DIM ** -0.5
    blk = ATTN_BLOCK
    rows = group * blk
    ri = lax.broadcasted_iota(jnp.int32, (rows, blk), 0) % blk
    ci = lax.broadcasted_iota(jnp.int32, (rows, blk), 1)
    ok_prev = jnp.logical_and(ci >= ri, qb > 0)
    ok_next = jnp.logical_and(ci <= ri, qb < nb - 1)
    cq, sq = cq_ref[...], sq_ref[...]
    for kh in range(kv_heads):
        ks = slice(kh * HEAD_DIM, (kh + 1) * HEAD_DIM)
        k_p = _rope(kp_ref[:, ks], cp_ref[...], sp_ref[...]).astype(BF16)
        k_c = _rope(kc_ref[:, ks], cq, sq).astype(BF16)
        k_n = _rope(kn_ref[:, ks], cn_ref[...], sn_ref[...]).astype(BF16)
        k_x = kx_ref[:, ks].astype(BF16)
        qs, sinks = [], []
        for g in range(group):
            hd = kh * group + g
            qs.append(_rope(q_ref[:, hd * HEAD_DIM:(hd + 1) * HEAD_DIM] * scale, cq, sq))
            sinks.append(jnp.full((blk, 1), sink_ref[hd], F32))
        q4 = jnp.concatenate(qs, axis=0).astype(BF16)
        sink = jnp.concatenate(sinks, axis=0)
        s_p = jnp.where(ok_prev, _dot_nt(q4, k_p), NEG)
        s_c = _dot_nt(q4, k_c)
        s_n = jnp.where(ok_next, _dot_nt(q4, k_n), NEG)
        s_x = _dot_nt(q4, k_x)
        m = jnp.maximum(jnp.maximum(jnp.max(s_p, -1, keepdims=True), jnp.max(s_c, -1, keepdims=True)),
                        jnp.maximum(jnp.max(s_n, -1, keepdims=True), jnp.max(s_x, -1, keepdims=True)))
        m = jnp.maximum(m, sink)
        p_p, p_c, p_n, p_x = jnp.exp(s_p - m), jnp.exp(s_c - m), jnp.exp(s_n - m), jnp.exp(s_x - m)
        den = (jnp.sum(p_p, -1, keepdims=True) + jnp.sum(p_c, -1, keepdims=True)
               + jnp.sum(p_n, -1, keepdims=True) + jnp.sum(p_x, -1, keepdims=True) + jnp.exp(sink - m))
        o4 = (_dot(p_p.astype(BF16), vp_ref[:, ks].astype(BF16)) + _dot(p_c.astype(BF16), vc_ref[:, ks].astype(BF16))
              + _dot(p_n.astype(BF16), vn_ref[:, ks].astype(BF16)) + _dot(p_x.astype(BF16), vx_ref[:, ks].astype(BF16)))
        o4 = o4 / den
        for g in range(group):
            hd = kh * group + g
            o_ref[:, hd * HEAD_DIM:(hd + 1) * HEAD_DIM] = o4[g * blk:(g + 1) * blk].astype(o_ref.dtype)


def _attn_ctx_kernel(sink_ref, q_ref, kx_ref, vx_ref, o_ref, *, kv_heads, group):
    scale = HEAD_DIM ** -0.5
    blk = q_ref.shape[0]
    for kh in range(kv_heads):
        ks = slice(kh * HEAD_DIM, (kh + 1) * HEAD_DIM)
        k_x = kx_ref[:, ks].astype(BF16)
        qs, sinks = [], []
        for g in range(group):
            hd = kh * group + g
            qs.append(q_ref[:, hd * HEAD_DIM:(hd + 1) * HEAD_DIM] * scale)
            sinks.append(jnp.full((blk, 1), sink_ref[hd], F32))
        q4 = jnp.concatenate(qs, axis=0).astype(BF16)
        sink = jnp.concatenate(sinks, axis=0)
        s_x = _dot_nt(q4, k_x)
        m = jnp.maximum(jnp.max(s_x, -1, keepdims=True), sink)
        p_x = jnp.exp(s_x - m)
        den = jnp.sum(p_x, -1, keepdims=True) + jnp.exp(sink - m)
        o4 = _dot(p_x.astype(BF16), vx_ref[:, ks].astype(BF16)) / den
        for g in range(group):
            hd = kh * group + g
            o_ref[:, hd * HEAD_DIM:(hd + 1) * HEAD_DIM] = o4[g * blk:(g + 1) * blk].astype(o_ref.dtype)


def _attention(lay, qkv, sink, n_heads, kv_heads, need_ctx):
    batch, seq, lc = lay.batch, lay.seq, lay.ctx_len
    group = n_heads // kv_heads
    a_q, a_kv = n_heads * HEAD_DIM, kv_heads * HEAD_DIM
    blk = ATTN_BLOCK
    nb = seq // blk
    qcol = a_q // a_kv
    ctx_blk0 = lay.n_lat // lc
    cos, sin = _rope_tables(seq)

    def kmap(shift, col):
        return lambda b, i, s: (b * nb + jnp.clip(i + shift, 0, nb - 1), col)

    def tmap(shift):
        return lambda b, i, s: (jnp.clip(i + shift, 0, nb - 1), 0)

    kspec = lambda shift, col: pl.BlockSpec((blk, a_kv), kmap(shift, col))
    tspec = lambda shift: pl.BlockSpec((blk, HEAD_DIM), tmap(shift))
    xspec = lambda col: pl.BlockSpec((lc, a_kv), lambda b, i, s: (ctx_blk0 + b, col))
    out_lat = pl.pallas_call(
        functools.partial(_attn_lat_kernel, kv_heads=kv_heads, group=group, nb=nb),
        out_shape=jax.ShapeDtypeStruct((lay.n_lat, a_q), BF16),
        grid_spec=pltpu.PrefetchScalarGridSpec(
            num_scalar_prefetch=1, grid=(batch, nb),
            in_specs=[pl.BlockSpec((blk, a_q), lambda b, i, s: (b * nb + i, 0)),
                      kspec(-1, qcol), kspec(0, qcol), kspec(1, qcol),
                      kspec(-1, qcol + 1), kspec(0, qcol + 1), kspec(1, qcol + 1),
                      xspec(qcol), xspec(qcol + 1),
                      tspec(0), tspec(0), tspec(-1), tspec(-1), tspec(1), tspec(1)],
            out_specs=pl.BlockSpec((blk, a_q), lambda b, i, s: (b * nb + i, 0))),
        compiler_params=_cparams(("parallel", "parallel")),
        name="attn_latent",
    )(sink, qkv, qkv, qkv, qkv, qkv, qkv, qkv, qkv, qkv, cos, sin, cos, sin, cos, sin)
    if not need_ctx:
        return out_lat
    cb = _pick(lc, (128, 64, 32, 16, 8))
    ncb = lc // cb
    row0 = lay.n_lat // cb
    out_ctx = pl.pallas_call(
        functools.partial(_attn_ctx_kernel, kv_heads=kv_heads, group=group),
        out_shape=jax.ShapeDtypeStruct((lay.n_ctx, a_q), BF16),
        grid_spec=pltpu.PrefetchScalarGridSpec(
            num_scalar_prefetch=1, grid=(batch, ncb),
            in_specs=[pl.BlockSpec((cb, a_q), lambda b, i, s: (row0 + b * ncb + i, 0)),
                      xspec(qcol), xspec(qcol + 1)],
            out_specs=pl.BlockSpec((cb, a_q), lambda b, i, s: (b * ncb + i, 0))),
        compiler_params=_cparams(("parallel", "parallel")),
        name="attn_context",
    )(sink, qkv, qkv, qkv)
    return jnp.concatenate([out_lat, out_ctx], axis=0)


def _post_value(y, x, g, gate):
    r = y * lax.rsqrt(jnp.mean(y * y, axis=-1, keepdims=True) + EPS)
    return x + gate * (r * g)


def _out_proj_kernel(a_ref, w_ref, x_ref, g_ref, gate_ref, o_ref, acc_ref):
    k = pl.program_id(1)

    @pl.when(k == 0)
    def _():
        acc_ref[...] = jnp.zeros_like(acc_ref)

    acc_ref[...] += _dot(a_ref[...], w_ref[0].astype(BF16))

    @pl.when(k == pl.num_programs(1) - 1)
    def _():
        o_ref[...] = _post_value(acc_ref[...], x_ref[...], g_ref[0], gate_ref[0])


def _gdn_out_proj_kernel(of_ref, ob_ref, z_ref, nw_ref, w_ref, x_ref, g_ref, gate_ref, o_ref, acc_ref):
    k = pl.program_id(1)

    @pl.when(k == 0)
    def _():
        acc_ref[...] = jnp.zeros_like(acc_ref)

    o = of_ref[...] + ob_ref[...]
    z = z_ref[...]
    nw = nw_ref[...]
    parts = []
    for h in range(o.shape[1] // DN_DV):
        sl = slice(h * DN_DV, (h + 1) * DN_DV)
        oh = o[:, sl]
        y = oh * lax.rsqrt(jnp.mean(oh * oh, axis=-1, keepdims=True) + EPS) * nw
        parts.append((y * _silu(z[:, sl])).astype(BF16))
    a = jnp.concatenate(parts, axis=1) if len(parts) > 1 else parts[0]
    acc_ref[...] += _dot(a, w_ref[0].astype(BF16))

    @pl.when(k == pl.num_programs(1) - 1)
    def _():
        o_ref[...] = _post_value(acc_ref[...], x_ref[...], g_ref[0], gate_ref[0])


def _out_proj(lay, a, w, wi, x, mods3, norm3, layer, which, k_gate, rows):
    _, kdim, d = w.shape
    tm = lay.row_tile((512, 256, 128, 64, 32, 16, 8))
    tk = _pick(kdim, (512, 256, 128))
    return pl.pallas_call(
        _out_proj_kernel,
        out_shape=jax.ShapeDtypeStruct((rows, d), F32),
        grid=(rows // tm, kdim // tk),
        in_specs=[pl.BlockSpec((tm, tk), lambda i, k: (i, k)),
                  pl.BlockSpec((1, tk, d), lambda i, k: (wi, k, 0)),
                  pl.BlockSpec((tm, d), lambda i, k: (i, 0)),
                  pl.BlockSpec((1, 1, d), lambda i, k: (layer * 4 + which, 0, 0)),
                  pl.BlockSpec((1, 1, d), lay.mod_index(layer, k_gate, tm))],
        out_specs=pl.BlockSpec((tm, d), lambda i, k: (i, 0)),
        scratch_shapes=[pltpu.VMEM((tm, d), F32)],
        compiler_params=_cparams(("parallel", "arbitrary")),
        name="out_proj",
    )(a, w, x, norm3, mods3)


def _gdn_out_proj(lay, o_f, o_b, p_all, z_col0, norm_w, w, wi, x, mods3, norm3, layer, which, k_gate, rows):
    _, kdim, d = w.shape
    tm = lay.row_tile((512, 256, 128, 64, 32, 16, 8))
    tk = _pick(math.gcd(kdim, z_col0), (512, 256, 128))
    zoff = z_col0 // tk
    return pl.pallas_call(
        _gdn_out_proj_kernel,
        out_shape=jax.ShapeDtypeStruct((rows, d), F32),
        grid=(rows // tm, kdim // tk),
        in_specs=[pl.BlockSpec((tm, tk), lambda i, k: (i, k)),
                  pl.BlockSpec((tm, tk), lambda i, k: (i, k)),
                  pl.BlockSpec((tm, tk), lambda i, k: (i, k + zoff)),
                  pl.BlockSpec((1, DN_DV), lambda i, k: (0, 0)),
                  pl.BlockSpec((1, tk, d), lambda i, k: (wi, k, 0)),
                  pl.BlockSpec((tm, d), lambda i, k: (i, 0)),
                  pl.BlockSpec((1, 1, d), lambda i, k: (layer * 4 + which, 0, 0)),
                  pl.BlockSpec((1, 1, d), lay.mod_index(layer, k_gate, tm))],
        out_specs=pl.BlockSpec((tm, d), lambda i, k: (i, 0)),
        scratch_shapes=[pltpu.VMEM((tm, d), F32)],
        compiler_params=_cparams(("parallel", "arbitrary")),
        name="gdn_out_proj",
    )(o_f, o_b, p_all, norm_w.reshape(1, DN_DV), w, x, norm3, mods3)


def _swiglu_step(h, wg_ref, wu_ref, wd_ref):
    gt = _dot(h, wg_ref[0].astype(BF16))
    up = _dot(h, wu_ref[0].astype(BF16))
    return _dot((_silu(gt) * up).astype(BF16), wd_ref[0].astype(BF16))


def _ffn_kernel(h_ref, wg_ref, wu_ref, wd_ref, x_ref, g_ref, gate_ref, o_ref, acc_ref):
    f = pl.program_id(1)

    @pl.when(f == 0)
    def _():
        acc_ref[...] = jnp.zeros_like(acc_ref)

    acc_ref[...] += _swiglu_step(h_ref[...], wg_ref, wu_ref, wd_ref)

    @pl.when(f == pl.num_programs(1) - 1)
    def _():
        o_ref[...] = _post_value(acc_ref[...], x_ref[...], g_ref[0], gate_ref[0])


def _ffn(lay, h, wg, wu, wd, wi, x, mods3, norm3, layer, which, k_gate, rows):
    _, d, dff = wg.shape
    tm = lay.row_tile((512, 256, 128, 64, 32, 16, 8))
    tf = _pick(dff, (256, 128))
    return pl.pallas_call(
        _ffn_kernel,
        out_shape=jax.ShapeDtypeStruct((rows, d), F32),
        grid=(rows // tm, dff // tf),
        in_specs=[pl.BlockSpec((tm, d), lambda i, f: (i, 0)),
                  pl.BlockSpec((1, d, tf), lambda i, f: (wi, 0, f)),
                  pl.BlockSpec((1, d, tf), lambda i, f: (wi, 0, f)),
                  pl.BlockSpec((1, tf, d), lambda i, f: (wi, f, 0)),
                  pl.BlockSpec((tm, d), lambda i, f: (i, 0)),
                  pl.BlockSpec((1, 1, d), lambda i, f: (layer * 4 + which, 0, 0)),
                  pl.BlockSpec((1, 1, d), lay.mod_index(layer, k_gate, tm))],
        out_specs=pl.BlockSpec((tm, d), lambda i, f: (i, 0)),
        scratch_shapes=[pltpu.VMEM((tm, d), F32)],
        compiler_params=_cparams(("parallel", "arbitrary")),
        name="ffn",
    )(h, wg, wu, wd, x, norm3, mods3)


def _moe_plan(route, n_e, tm):
    t = route.shape[0]
    e_flat = jnp.concatenate([route[:, 0], route[:, 1]]).astype(jnp.int32)
    onehot = (e_flat[:, None] == jnp.arange(n_e, dtype=jnp.int32)[None, :]).astype(jnp.int32)
    rank = jnp.sum((jnp.cumsum(onehot, axis=0) - onehot) * onehot, axis=1)
    counts = jnp.sum(onehot, axis=0)
    tiles_e = (counts + tm - 1) // tm
    tile_end = jnp.cumsum(tiles_e)
    tile_start = tile_end - tiles_e
    dest = tile_start[e_flat] * tm + rank
    n_tiles = (TOP_K * t + n_e * (tm - 1) + tm - 1) // tm
    n_used = tile_end[-1]
    tile_ids = jnp.arange(n_tiles, dtype=jnp.int32)
    active = (tile_ids < n_used).astype(jnp.int32)
    clamped = jnp.minimum(tile_ids, n_used - 1)
    tile_expert = jnp.sum((clamped[:, None] >= tile_end[None, :]).astype(jnp.int32), axis=1)
    token = jnp.arange(TOP_K * t, dtype=jnp.int32) % t
    src = jnp.zeros((n_tiles * tm,), jnp.int32).at[dest].set(token)
    return dest, src.reshape(n_tiles, 1, tm), tile_expert, active, clamped, n_tiles


def _moe_ffn_kernel(te_ref, ta_ref, tc_ref, src_ref, h3_ref, wg_ref, wu_ref, wd_ref, o_ref,
                    gbuf_ref, xb_ref, acc_ref, sem):
    i, f = pl.program_id(0), pl.program_id(1)
    tm, slabs = gbuf_ref.shape[0], gbuf_ref.shape[1]
    active = ta_ref[i] == 1
    last = f == pl.num_programs(1) - 1

    @pl.when(jnp.logical_and(active, f == 0))
    def _():
        def issue(r, carry):
            pltpu.make_async_copy(h3_ref.at[src_ref[0, 0, r]], gbuf_ref.at[r], sem).start()
            return carry
        lax.fori_loop(0, tm, issue, 0)
        pltpu.make_async_copy(h3_ref.at[pl.ds(0, tm)], gbuf_ref, sem).wait()
        for s in range(slabs):
            xb_ref[:, s * LANES:(s + 1) * LANES] = gbuf_ref[:, s, :].astype(BF16)
        acc_ref[...] = jnp.zeros_like(acc_ref)

    @pl.when(active)
    def _():
        acc_ref[...] += _swiglu_step(xb_ref[...], wg_ref, wu_ref, wd_ref)

    @pl.when(jnp.logical_and(active, last))
    def _():
        for s in range(slabs):
            o_ref[:, s, :] = acc_ref[:, s * LANES:(s + 1) * LANES]

    @pl.when(jnp.logical_and(jnp.logical_not(active), last))
    def _():
        o_ref[...] = jnp.zeros_like(o_ref)


def _moe_ffn(h3, plan, wg, wu, wd, w0):
    _, src, tile_expert, active, clamped, n_tiles = plan
    _, d, dff = wg.shape
    slabs = d // LANES
    tm = src.shape[2]
    tf = _pick(dff, (256, 128))
    nf = dff // tf

    def wmap_in(i, f, te, ta, tc):
        return (w0 + te[i], 0, jnp.where(ta[i] == 1, f, nf - 1))

    def wmap_out(i, f, te, ta, tc):
        return (w0 + te[i], jnp.where(ta[i] == 1, f, nf - 1), 0)

    return pl.pallas_call(
        _moe_ffn_kernel,
        out_shape=jax.ShapeDtypeStruct((n_tiles * tm, slabs, LANES), F32),
        grid_spec=pltpu.PrefetchScalarGridSpec(
            num_scalar_prefetch=3, grid=(n_tiles, nf),
            in_specs=[pl.BlockSpec((1, 1, tm), lambda i, f, te, ta, tc: (tc[i], 0, 0), memory_space=pltpu.SMEM),
                      pl.BlockSpec(memory_space=pl.ANY),
                      pl.BlockSpec((1, d, tf), wmap_in),
                      pl.BlockSpec((1, d, tf), wmap_in),
                      pl.BlockSpec((1, tf, d), wmap_out)],
            out_specs=pl.BlockSpec((tm, slabs, LANES), lambda i, f, te, ta, tc: (i, 0, 0)),
            scratch_shapes=[pltpu.VMEM((tm, slabs, LANES), F32), pltpu.VMEM((tm, d), BF16),
                            pltpu.VMEM((tm, d), F32), pltpu.SemaphoreType.DMA(())]),
        compiler_params=_cparams(("arbitrary", "arbitrary")),
        name="moe_ffn",
    )(tile_expert, active, clamped, src, h3, wg, wu, wd)


def _moe_combine_kernel(dest_ref, route_ref, y3_ref, x_ref, g_ref, gate_ref, o_ref, b0_ref, b1_ref, sem):
    tm, slabs = b0_ref.shape[0], b0_ref.shape[1]

    def issue(r, carry):
        pltpu.make_async_copy(y3_ref.at[dest_ref[0, 0, r]], b0_ref.at[r], sem.at[0]).start()
        pltpu.make_async_copy(y3_ref.at[dest_ref[0, 1, r]], b1_ref.at[r], sem.at[1]).start()
        return carry
    lax.fori_loop(0, tm, issue, 0)
    pltpu.make_async_copy(y3_ref.at[pl.ds(0, tm)], b0_ref, sem.at[0]).wait()
    pltpu.make_async_copy(y3_ref.at[pl.ds(0, tm)], b1_ref, sem.at[1]).wait()
    w1, w2 = route_ref[:, 2:3], route_ref[:, 3:4]
    ys = [w1 * b0_ref[:, s, :] + w2 * b1_ref[:, s, :] for s in range(slabs)]
    ss = ys[0] * ys[0]
    for s in range(1, slabs):
        ss = ss + ys[s] * ys[s]
    inv = lax.rsqrt(jnp.sum(ss, axis=-1, keepdims=True) / (slabs * LANES) + EPS)
    g, gate = g_ref[0], gate_ref[0]
    for s in range(slabs):
        sl = slice(s * LANES, (s + 1) * LANES)
        o_ref[:, sl] = x_ref[:, sl] + gate[:, sl] * ((ys[s] * inv) * g[:, sl])


def _moe_combine(lay, y3, dest, route, x, mods3, norm3, layer, which, k_gate, rows):
    d = lay.d
    slabs = d // LANES
    tm = lay.row_tile((256, 128, 64, 32, 16, 8))
    dest3 = dest.reshape(TOP_K, rows // tm, tm).transpose(1, 0, 2)
    return pl.pallas_call(
        _moe_combine_kernel,
        out_shape=jax.ShapeDtypeStruct((rows, d), F32),
        grid=(rows // tm,),
        in_specs=[pl.BlockSpec((1, TOP_K, tm), lambda i: (i, 0, 0), memory_space=pltpu.SMEM),
                  pl.BlockSpec((tm, LANES), lambda i: (i, 0)),
                  pl.BlockSpec(memory_space=pl.ANY),
                  pl.BlockSpec((tm, d), lambda i: (i, 0)),
                  pl.BlockSpec((1, 1, d), lambda i: (layer * 4 + which, 0, 0)),
                  pl.BlockSpec((1, 1, d), lay.mod_index(layer, k_gate, tm))],
        out_specs=pl.BlockSpec((tm, d), lambda i: (i, 0)),
        scratch_shapes=[pltpu.VMEM((tm, slabs, LANES), F32), pltpu.VMEM((tm, slabs, LANES), F32),
                        pltpu.SemaphoreType.DMA((2,))],
        compiler_params=_cparams(("arbitrary",)),
        name="moe_combine",
    )(dest3, route, y3, x, norm3, mods3)


def _gdn_conv_kernel(cur_ref, prev_ref, next_ref, w_ref, o_ref, *, lay, tr, tc, qk_cols, q_cols):
    i, j = pl.program_id(0), pl.program_id(1)
    r0 = i * tr
    in_lat = r0 < lay.n_lat
    pos = jnp.where(in_lat, r0 % lay.seq, (r0 - lay.n_lat) % lay.ctx_len)
    seg = jnp.where(in_lat, lay.seq, lay.ctx_len)
    has_prev = (pos > 0).astype(F32)
    has_next = (pos + tr < seg).astype(F32)
    x = cur_ref[...]
    prev = prev_ref[...] * has_prev
    nxt = next_ref[...] * has_next
    w = w_ref[...]
    row = lax.broadcasted_iota(jnp.int32, x.shape, 0)
    xm2 = jnp.where(row == 0, prev[6:7], jnp.where(row == 1, prev[7:8], pltpu.roll(x, 2, 0)))
    xm1 = jnp.where(row == 0, prev[7:8], pltpu.roll(x, 1, 0))
    xp1 = jnp.where(row == tr - 1, nxt[0:1], pltpu.roll(x, tr - 1, 0))
    xp2 = jnp.where(row == tr - 2, nxt[0:1], jnp.where(row == tr - 1, nxt[1:2], pltpu.roll(x, tr - 2, 0)))
    y = w[0:1] * xm2 + w[1:2] * xm1 + w[2:3] * x + w[3:4] * xp1 + w[4:5] * xp2
    y = _silu(y)
    c0 = j * tc

    @pl.when(c0 >= qk_cols)
    def _():
        o_ref[...] = y

    @pl.when(c0 < qk_cols)
    def _():
        qscale = jnp.where(c0 < q_cols, DN_DK ** -0.5, 1.0).astype(F32)
        for hh in range(tc // DN_DK):
            sl = slice(hh * DN_DK, (hh + 1) * DN_DK)
            yh = y[:, sl]
            o_ref[:, sl] = yh * (lax.rsqrt(jnp.sum(yh * yh, axis=-1, keepdims=True) + EPS) * qscale)


def _gdn_conv(lay, p_all, conv_w, n_qkv, n_k):
    tr = _pick(math.gcd(lay.seq, lay.ctx_len), (256, 128, 64, 32, 16, 8))
    tc = _pick(math.gcd(n_qkv, n_k), (512, 256, 128))
    hb = tr // 8
    n8 = lay.n_all // 8
    return pl.pallas_call(
        functools.partial(_gdn_conv_kernel, lay=lay, tr=tr, tc=tc, qk_cols=2 * n_k, q_cols=n_k),
        out_shape=jax.ShapeDtypeStruct((lay.n_all, n_qkv), F32),
        grid=(lay.n_all // tr, n_qkv // tc),
        in_specs=[pl.BlockSpec((tr, tc), lambda i, j: (i, j)),
                  pl.BlockSpec((8, tc), lambda i, j: (jnp.maximum(i * hb - 1, 0), j)),
                  pl.BlockSpec((8, tc), lambda i, j: (jnp.minimum((i + 1) * hb, n8 - 1), j)),
                  pl.BlockSpec((SHORT_CONV, tc), lambda i, j: (0, j))],
        out_specs=pl.BlockSpec((tr, tc), lambda i, j: (i, j)),
        compiler_params=_cparams(("parallel", "parallel")),
        name="gdn_conv",
    )(p_all, p_all, p_all, conv_w)


def _softplus(x):
    return jnp.maximum(x, 0.0) + jnp.log1p(jnp.exp(-jnp.abs(x)))


def _gdn_gate_kernel(ab_ref, a_ref, dt_ref, o_ref, *, hv):
    n, width = ab_ref.shape
    lane = lax.broadcasted_iota(jnp.int32, (CHUNK, width), 1)
    is_g = (lane // hv) % 2 == 0
    is_bwd = lane >= 2 * hv
    ri = lax.broadcasted_iota(jnp.int32, (CHUNK, CHUNK), 0)
    ci = lax.broadcasted_iota(jnp.int32, (CHUNK, CHUNK), 1)
    lower = (ri >= ci).astype(F32)
    upper = (ri <= ci).astype(F32)
    for c in range(n // CHUNK):
        rs = slice(c * CHUNK, (c + 1) * CHUNK)
        x = ab_ref[rs, :]
        g = -a_ref[...] * _softplus(x + dt_ref[...])
        beta = 1.0 / (1.0 + jnp.exp(-x))
        cum = jnp.where(is_bwd, _dot_hi(upper, g), _dot_hi(lower, g))
        o_ref[rs, :] = jnp.where(is_g, cum, beta)


def _gdn_gates(lay, ab, a_log, dt_bias, hv):
    tr = lay.row_tile((256, 128, 64))
    width = 4 * hv
    zeros = jnp.zeros((hv,), F32)
    a_row = jnp.concatenate([jnp.exp(a_log[0].astype(F32)), zeros, jnp.exp(a_log[1].astype(F32)), zeros])
    dt_row = jnp.concatenate([dt_bias[0].astype(F32), zeros, dt_bias[1].astype(F32), zeros])
    return pl.pallas_call(
        functools.partial(_gdn_gate_kernel, hv=hv),
        out_shape=jax.ShapeDtypeStruct((lay.n_all, width), F32),
        grid=(lay.n_all // tr,),
        in_specs=[pl.BlockSpec((tr, width), lambda i: (i, 0)),
                  pl.BlockSpec((1, width), lambda i: (0, 0)),
                  pl.BlockSpec((1, width), lambda i: (0, 0))],
        out_specs=pl.BlockSpec((tr, width), lambda i: (i, 0)),
        compiler_params=_cparams(("parallel",)),
        name="gdn_gates",
    )(ab, a_row.reshape(1, width), dt_row.reshape(1, width))


def _bd_pair(z):
    n = z.shape[1]
    mask = (lax.broadcasted_iota(jnp.int32, (n, n), 0) // (n // 2)
            == lax.broadcasted_iota(jnp.int32, (n, n), 1) // (n // 2))
    return jnp.where(mask, jnp.concatenate([z, z], axis=0), 0.0)


def _chunk_inverses(ms):
    c, sb = CHUNK, SOLVE_BLOCK
    nblk = c // sb
    lane_p = lax.broadcasted_iota(jnp.int32, (sb, 2 * c), 1)
    row_p = lax.broadcasted_iota(jnp.int32, (sb, 2 * c), 0)
    col_p = lane_p % sb
    blk_p = (lane_p // sb) % nblk
    bd_mask_p = (lax.broadcasted_iota(jnp.int32, (2 * c, 2 * c), 0) // sb
                 == lax.broadcasted_iota(jnp.int32, (2 * c, 2 * c), 1) // sb)

    def bd_packed(z):
        return jnp.where(bd_mask_p, jnp.concatenate([z] * (2 * c // sb), axis=0), 0.0)

    def pack(m):
        mp = jnp.where(blk_p == 0, m[0:sb, :], 0.0)
        for r in range(1, nblk):
            mp = mp + jnp.where(blk_p == r, m[r * sb:(r + 1) * sb, :], 0.0)
        return mp

    mps = [pack(m) for m in ms]
    eye_p = (row_p == col_p).astype(F32)
    ds = [eye_p - jnp.where(row_p // 2 == col_p // 2, mp, 0.0) for mp in mps]
    s = 2
    while s < sb:
        sel = jnp.logical_and(row_p // (2 * s) == col_p // (2 * s), row_p // s != col_p // s)
        xs = [_dot3(jnp.where(sel, mp, 0.0), *_split(bd_packed(d))) for mp, d in zip(mps, ds)]
        ds = [d - _dot3(d, *_split(bd_packed(x))) for d, x in zip(ds, xs)]
        s *= 2
    ti = lax.broadcasted_iota(jnp.int32, (c, 2 * c), 0)
    tj = lax.broadcasted_iota(jnp.int32, (c, 2 * c), 1) % c
    same_blk = ti // sb == tj // sb
    fulls = [jnp.where(same_blk, jnp.concatenate([d] * nblk, axis=0), 0.0) for d in ds]
    while s < c:
        sel = jnp.logical_and(ti // (2 * s) == tj // (2 * s), ti // s != tj // s)
        xs = [_dot(jnp.where(sel, m, 0.0).astype(BF16), _bd_pair(fl).astype(BF16)) for m, fl in zip(ms, fulls)]
        fulls = [fl - _dot(fl.astype(BF16), _bd_pair(x).astype(BF16)) for fl, x in zip(fulls, xs)]
        s *= 2
    return fulls


def _gdn_chunk_kernel(q_ref, k_ref, v_ref, gcol_ref, grow_ref, o_ref, s_ref, *, reverse, n_chunks, n_pairs):
    step = pl.program_id(2)

    @pl.when(step == 0)
    def _():
        s_ref[...] = jnp.zeros_like(s_ref)

    c = CHUNK
    lane = lax.broadcasted_iota(jnp.int32, (c, 2 * c), 1)
    left = lane < c
    ti = lax.broadcasted_iota(jnp.int32, (c, 2 * c), 0)
    tj = lane % c
    if reverse:
        incl, strict = ti <= tj, ti < tj
    else:
        incl, strict = ti >= tj, ti > tj

    def side(col_a, col_b):
        return jnp.where(left, col_a, col_b)

    order = list(range(n_chunks - 1, -1, -1) if reverse else range(n_chunks))
    last = 0 if reverse else c - 1
    items = [(pr, ch) for ch in order for pr in range(n_pairs)]
    ks, qs, gates, kqs = [], [], [], []
    for pr, ch in items:
        rs = slice(ch * c, (ch + 1) * c)
        ksl = slice(pr * DN_DK, (pr + 1) * DN_DK)
        k, q = k_ref[rs, ksl], q_ref[rs, ksl]
        gcol = gcol_ref[pr, 0, rs, :]
        grow = jnp.concatenate([grow_ref[pr, 0, 0:1, rs], grow_ref[pr, 0, 1:2, rs]], axis=1)
        kb16 = k.astype(BF16)
        ks.append(k)
        qs.append(q)
        gates.append((gcol[:, 0:1], gcol[:, 1:2], gcol[:, 2:3], gcol[:, 3:4], grow))
        kqs.append(_dot_nt(jnp.concatenate([kb16, q.astype(BF16)], axis=0), jnp.concatenate([kb16, kb16], axis=0)))
    ms, qkds = [], []
    for kq, (g_a, g_b, b_a, b_b, grow) in zip(kqs, gates):
        decay = jnp.where(incl, jnp.exp(jnp.minimum(side(g_a, g_b) - grow, 0.0)), 0.0)
        ms.append(jnp.where(strict, kq[:c] * decay * side(b_a, b_b), 0.0))
        qkds.append((kq[c:] * decay).astype(BF16))
    t_invs = _chunk_inverses(ms)
    sols, egs = [], []
    for (pr, ch), k, (g_a, g_b, b_a, b_b, _), t_inv in zip(items, ks, gates, t_invs):
        rs = slice(ch * c, (ch + 1) * c)
        eg_a, eg_b = jnp.exp(g_a), jnp.exp(g_b)
        v0 = pr * 2 * DN_DV
        rhs = jnp.concatenate(
            [jnp.concatenate([v_ref[rs, v0:v0 + DN_DV] * b_a, k * (b_a * eg_a)], axis=1),
             jnp.concatenate([v_ref[rs, v0 + DN_DV:v0 + 2 * DN_DV] * b_b, k * (b_b * eg_b)], axis=1)], axis=0)
        sols.append(_dot(_bd_pair(t_inv).astype(BF16), rhs.astype(BF16)))
        egs.append((eg_a, eg_b))
    for pos, ch in enumerate(order):
        rs = slice(ch * c, (ch + 1) * c)
        heads = []
        for pr in range(n_pairs):
            it = pos * n_pairs + pr
            for hd in range(2):
                heads.append((it, hd, pr * 2 + hd, gates[it][hd], egs[it][hd]))
        s_olds = [s_ref[hi] for _, _, hi, _, _ in heads]
        wss = [_dot(jnp.concatenate([sols[it][hd * c:(hd + 1) * c, DN_DV:], qs[it] * eg_h], axis=0).astype(BF16),
                    s_old.astype(BF16))
               for (it, hd, _, _, eg_h), s_old in zip(heads, s_olds)]
        vns = [(sols[it][hd * c:(hd + 1) * c, 0:DN_DV] - ws[:c]).astype(BF16)
               for (it, hd, _, _, _), ws in zip(heads, wss)]
        for (it, hd, hi, g_h, _), ws, vn16, s_old in zip(heads, wss, vns, s_olds):
            g_last = g_h[last:last + 1, :]
            intra = _dot(qkds[it][:, hd * c:(hd + 1) * c], vn16)
            o_ref[rs, hi * DN_DV:(hi + 1) * DN_DV] = (ws[c:] + intra).astype(o_ref.dtype)
            kd = (ks[it] * jnp.exp(g_last - g_h)).astype(BF16)
            s_ref[hi] = s_old * jnp.exp(g_last) + _dot_tn(kd, vn16)


def _gdn_scan(lay, qkv, gcol, grow, hk, hv, reverse):
    rb = _pick(lay.ctx_len, (256, 128, 64))
    n_chunks = rb // CHUNK
    n_cb = lay.ctx_len // rb
    n_lb = lay.seq // rb
    steps = n_cb + n_lb
    ctx0 = lay.n_lat // rb
    assert hv // hk == 2, "the scan kernel pairs the two value heads of each key head"
    pp = _pick(hk, (2, 1))
    ng = hk // pp

    def rowblock(b, s):
        if reverse:
            return jnp.where(s < n_cb, ctx0 + b * n_cb + (n_cb - 1 - s), b * n_lb + (n_lb - 1 - (s - n_cb)))
        return jnp.where(s < n_cb, ctx0 + b * n_cb + s, b * n_lb + (s - n_cb))

    return pl.pallas_call(
        functools.partial(_gdn_chunk_kernel, reverse=reverse, n_chunks=n_chunks, n_pairs=pp),
        out_shape=jax.ShapeDtypeStruct((lay.n_all, hv * DN_DV), F32),
        grid=(lay.batch, ng, steps),
        in_specs=[pl.BlockSpec((rb, pp * DN_DK), lambda b, p, s: (rowblock(b, s), p)),
                  pl.BlockSpec((rb, pp * DN_DK), lambda b, p, s: (rowblock(b, s), ng + p)),
                  pl.BlockSpec((rb, pp * 2 * DN_DV), lambda b, p, s: (rowblock(b, s), ng + p)),
                  pl.BlockSpec((pp, 1, rb, 4), lambda b, p, s: (p, 0, rowblock(b, s), 0)),
                  pl.BlockSpec((pp, 1, 8, rb), lambda b, p, s: (p, 0, 0, rowblock(b, s)))],
        out_specs=pl.BlockSpec((rb, pp * 2 * DN_DV), lambda b, p, s: (rowblock(b, s), p)),
        scratch_shapes=[pltpu.VMEM((pp * 2, DN_DK, DN_DV), F32)],
        compiler_params=_cparams(("parallel", "parallel", "arbitrary")),
        name="gdn_scan_bwd" if reverse else "gdn_scan_fwd",
    )(qkv, qkv, qkv, gcol, grow)


def _gate_layouts(gates, hv, direction):
    n = gates.shape[0]
    g = gates[:, direction * 2 * hv: direction * 2 * hv + hv].reshape(n, hv // 2, 2)
    b = gates[:, direction * 2 * hv + hv: (direction + 1) * 2 * hv].reshape(n, hv // 2, 2)
    col = jnp.concatenate([g, b], axis=2).transpose(1, 0, 2)[:, None]
    row = jnp.pad(g.transpose(1, 2, 0), ((0, 0), (0, 6), (0, 0)))[:, None]
    return col, row


def kernel(x, c, ctx, c_ctx, mod_w, mod_b, norm_g, attn_w_qkv, attn_w_o, attn_sink, dn_w_in, dn_conv, dn_a_log,
           dn_dt_bias, dn_norm_g, dn_w_o, ffn_w_gate, ffn_w_up, ffn_w_down, moe_router, moe_w_gate, moe_w_up,
           moe_w_down):
    batch, seq, d = x.shape
    lc = ctx.shape[1]
    depth = mod_w.shape[0]
    seg_rows = -(-(batch + 1) // 8) * 8
    lay = _Layout(batch, seq, lc, d, seg_rows)

    n_heads = attn_sink.shape[1]
    kv_heads = (attn_w_qkv.shape[2] // HEAD_DIM - n_heads) // 2
    hv = dn_a_log.shape[2]
    hk = (dn_w_in.shape[2] - 4 * hv - 2 * hv * DN_DV) // (2 * DN_DK)
    n_k, n_v = hk * DN_DK, hv * DN_DV
    n_qkv = 2 * n_k + n_v
    n_e, dff_e = moe_w_gate.shape[1], moe_w_gate.shape[3]
    moe_wg = moe_w_gate.reshape(-1, d, dff_e)
    moe_wu = moe_w_up.reshape(-1, d, dff_e)
    moe_wd = moe_w_down.reshape(-1, dff_e, d)

    cvec = jnp.concatenate([c, c_ctx[None], jnp.zeros((seg_rows - batch - 1, d), F32)], axis=0)
    mods3 = _modulation(cvec, mod_w, mod_b).reshape(depth * seg_rows * N_MOD, 1, d)
    norm3 = norm_g.reshape(depth * 4, 1, d)

    xs = jnp.concatenate([x.reshape(batch * seq, d), ctx.reshape(batch * lc, d)], axis=0)
    for i in range(depth):
        need_ctx = i < depth - 1
        rows = lay.n_all if need_ctx else lay.n_lat
        j = i // 2
        h = _norm_mod(lay, xs, mods3, norm3, i, 0, 0, 1, lay.n_all)
        if i % 2 == 0:
            qkv = _project(h, attn_w_qkv, j, 0, attn_w_qkv.shape[2], F32)
            a = _attention(lay, qkv, attn_sink[j], n_heads, kv_heads, need_ctx)
            xs = _out_proj(lay, a, attn_w_o, j, xs, mods3, norm3, i, 1, 2, rows)
            h = _norm_mod(lay, xs, mods3, norm3, i, 2, 3, 4, rows)
            xs = _ffn(lay, h, ffn_w_gate, ffn_w_up, ffn_w_down, j, xs, mods3, norm3, i, 3, 5, rows)
        else:
            p_all = _project(h, dn_w_in, j, 0, n_qkv + n_v, F32)
            ab = _project(h, dn_w_in, j, n_qkv + n_v, 4 * hv, F32)
            qkv = _gdn_conv(lay, p_all, dn_conv[j], n_qkv, n_k)
            gates = _gdn_gates(lay, ab, dn_a_log[j], dn_dt_bias[j], hv)
            o_f = _gdn_scan(lay, qkv, *_gate_layouts(gates, hv, 0), hk, hv, False)
            o_b = _gdn_scan(lay, qkv, *_gate_layouts(gates, hv, 1), hk, hv, True)
            xs = _gdn_out_proj(lay, o_f, o_b, p_all, n_qkv, dn_norm_g[j], dn_w_o, j, xs, mods3, norm3,
                               i, 1, 2, rows)
            h3, route = _norm_mod(lay, xs, mods3, norm3, i, 2, 3, 4, rows, router=moe_router[j])
            plan = _moe_plan(route, n_e, MOE_TILE)
            y3 = _moe_ffn(h3, plan, moe_wg, moe_wu, moe_wd, j * n_e)
            xs = _moe_combine(lay, y3, plan[0], route, xs, mods3, norm3, i, 3, 5, rows)
    return xs[:lay.n_lat].reshape(batch, seq, d)
```
